```python
import math
import jax, jax.numpy as jnp
from jax import lax
import numpy as np

D_MODEL = 1024
BATCH = 8
SEQ = 2048
DEPTH = 1
DEC_BATCH = 128
DEC_SEQ = 1
PAST_LEN = 16384
PAGE_SIZE = 128

W_A = D_MODEL
N_A_GROUPS = 8
W_B = D_MODEL
N_B_HEADS = 8
HEAD_B = W_B // N_B_HEADS
W_MIX = W_A + W_B
W_IN = 3 * W_A + 2 * W_B
CONV_A_WIDTH = 31
CONV_B_WIDTH = 4
RG_C = 8.0
PLE_DIM = 256
EPS = 1e-6

kernel_name = "hymba_conformer_rglru_step"


def _rmsnorm(x, g):
    xf = x.astype(jnp.float32)
    y = xf * lax.rsqrt(jnp.mean(xf * xf, axis=-1, keepdims=True) + EPS)
    return (y * g.astype(jnp.float32)).astype(x.dtype)


def _layernorm(x, g, b):
    xf = x.astype(jnp.float32)
    mu = jnp.mean(xf, axis=-1, keepdims=True)
    xc = xf - mu
    var = jnp.mean(xc * xc, axis=-1, keepdims=True)
    y = xc * lax.rsqrt(var + EPS) * g.astype(jnp.float32) + b.astype(jnp.float32)
    return y.astype(x.dtype)


def _causal_dwconv(x, buf, w, b):
    k = w.shape[0]
    xp = jnp.concatenate([buf.astype(x.dtype), x], axis=1)
    y = lax.conv_general_dilated(
        xp, w[:, None, :].astype(x.dtype), window_strides=(1,), padding='VALID',
        dimension_numbers=('NWC', 'WIO', 'NWC'), feature_group_count=x.shape[-1])
    new_buf = xp[:, xp.shape[1] - (k - 1):]
    return y + b.astype(x.dtype), new_buf


def _rglru(x, h0, w_r, b_r, w_i, b_i, lam):
    bsz, t, w = x.shape
    xh = x.reshape(bsz, t, N_B_HEADS, HEAD_B)
    r = jax.nn.sigmoid(jnp.einsum('bthi,hij->bthj', xh, w_r.astype(x.dtype)).reshape(bsz, t, w) + b_r.astype(x.dtype))
    i = jax.nn.sigmoid(jnp.einsum('bthi,hij->bthj', xh, w_i.astype(x.dtype)).reshape(bsz, t, w) + b_i.astype(x.dtype))
    log_a = -RG_C * r.astype(jnp.float32) * jax.nn.softplus(-lam.astype(jnp.float32))
    a = jnp.exp(log_a)
    bx = jnp.sqrt(-jnp.expm1(2.0 * log_a)) * (i * x).astype(jnp.float32)

    def step(hc, ab):
        a_t, b_t = ab
        hc = a_t * hc + b_t
        return hc, hc

    h_last, hs = lax.scan(step, h0.astype(jnp.float32), (jnp.swapaxes(a, 0, 1), jnp.swapaxes(bx, 0, 1)))
    return jnp.swapaxes(hs, 0, 1).astype(x.dtype), h_last


def _layer(h, p, buf_a, buf_b, h0, g_norm, w_in, w_dw_a, b_dw_a, ln_g, ln_b,
           w_conv_b, b_conv_b, w_r, b_r, w_i, b_i, lam, w_out, w_pe, w_pg):
    u = _rmsnorm(h, g_norm)
    z = u @ w_in.astype(u.dtype)
    a_val = z[..., :W_A]
    a_glu = z[..., W_A:2 * W_A]
    a_gate = z[..., 2 * W_A:3 * W_A]
    b_x = z[..., 3 * W_A:3 * W_A + W_B]
    b_gate = z[..., 3 * W_A + W_B:]
    v = a_val * jax.nn.sigmoid(a_glu)
    v, new_a = _causal_dwconv(v, buf_a, w_dw_a, b_dw_a)
    v = jax.nn.silu(_layernorm(v, ln_g, ln_b)) * jax.nn.silu(a_gate)
    xb, new_b = _causal_dwconv(b_x, buf_b, w_conv_b, b_conv_b)
    yb, h_last = _rglru(xb, h0, w_r, b_r, w_i, b_i, lam)
    yb = yb * jax.nn.silu(b_gate)
    mix = jnp.concatenate([v, yb], axis=-1) @ w_out.astype(u.dtype)
    h = h + mix
    pe = (p.astype(h.dtype) @ w_pe.astype(h.dtype)) * jax.nn.sigmoid(h @ w_pg.astype(h.dtype))
    h = h + pe
    return h, new_a, new_b, h_last.astype(h0.dtype)


def _trunk(x, p, bufs_a, bufs_b, hs0, g_norm, w_in, w_dw_a, b_dw_a, ln_g, ln_b,
           w_conv_b, b_conv_b, w_r, b_r, w_i, b_i, lam, w_out, w_pe, w_pg, g_final):
    h = x
    na, nb, nh = [], [], []
    for l in range(DEPTH):
        h, a_, b_, h_ = _layer(h, p[l], bufs_a[l], bufs_b[l], hs0[l], g_norm[l], w_in[l],
                               w_dw_a[l], b_dw_a[l], ln_g[l], ln_b[l], w_conv_b[l], b_conv_b[l],
                               w_r[l], b_r[l], w_i[l], b_i[l], lam[l], w_out[l], w_pe[l], w_pg[l])
        na.append(a_)
        nb.append(b_)
        nh.append(h_)
    y = _rmsnorm(h, g_final)
    return y, jnp.stack(na, 0), jnp.stack(nb, 0), jnp.stack(nh, 0)


def setup_inputs(seed: int = 0) -> dict:
    key = jax.random.key(seed)
    ks = jax.random.split(key, 24)
    f32 = jnp.float32
    nrm = lambda k, s, sc: jax.random.normal(k, s, f32) * sc
    u = jax.random.uniform(ks[20], (DEPTH, W_B), f32, 0.9, 0.999)
    s = u ** (1.0 / RG_C)
    lam = jnp.log(s) - jnp.log1p(-s)
    return {
        "x_prompt": nrm(ks[0], (BATCH, SEQ, D_MODEL), 1.0),
        "x_sample": nrm(ks[1], (DEC_BATCH, DEC_SEQ, D_MODEL), 1.0),
        "p_prompt": nrm(ks[2], (DEPTH, BATCH, SEQ, PLE_DIM), 1.0),
        "p_sample": nrm(ks[3], (DEPTH, DEC_BATCH, DEC_SEQ, PLE_DIM), 1.0),
        "state_conv_a": nrm(ks[4], (DEPTH, DEC_BATCH, CONV_A_WIDTH - 1, W_A), 0.5),
        "state_conv_b": nrm(ks[5], (DEPTH, DEC_BATCH, CONV_B_WIDTH - 1, W_B), 1.0),
        "state_h": nrm(ks[6], (DEPTH, DEC_BATCH, W_B), 0.5),
        "g_norm": 1.0 + nrm(ks[7], (DEPTH, D_MODEL), 0.02),
        "w_in": nrm(ks[8], (DEPTH, D_MODEL, W_IN), D_MODEL ** -0.5),
        "w_dw_a": nrm(ks[9], (DEPTH, CONV_A_WIDTH, W_A), CONV_A_WIDTH ** -0.5),
        "b_dw_a": nrm(ks[10], (DEPTH, W_A), 0.02),
        "ln_g": 1.0 + nrm(ks[11], (DEPTH, W_A), 0.02),
        "ln_b": nrm(ks[12], (DEPTH, W_A), 0.02),
        "w_conv_b": nrm(ks[13], (DEPTH, CONV_B_WIDTH, W_B), CONV_B_WIDTH ** -0.5),
        "b_conv_b": nrm(ks[14], (DEPTH, W_B), 0.02),
        "w_r": nrm(ks[15], (DEPTH, N_B_HEADS, HEAD_B, HEAD_B), HEAD_B ** -0.5),
        "b_r": nrm(ks[16], (DEPTH, W_B), 0.02),
        "w_i": nrm(ks[17], (DEPTH, N_B_HEADS, HEAD_B, HEAD_B), HEAD_B ** -0.5),
        "b_i": nrm(ks[18], (DEPTH, W_B), 0.02),
        "lam": lam,
        "w_out": nrm(ks[19], (DEPTH, W_MIX, D_MODEL), W_MIX ** -0.5),
        "w_pe": nrm(ks[21], (DEPTH, PLE_DIM, D_MODEL), PLE_DIM ** -0.5),
        "w_pg": nrm(ks[22], (DEPTH, D_MODEL, D_MODEL), D_MODEL ** -0.5),
        "g_final": 1.0 + nrm(ks[23], (D_MODEL,), 0.02),
    }


def reference(x_prompt, x_sample, p_prompt, p_sample, state_conv_a, state_conv_b, state_h,
              g_norm, w_in, w_dw_a, b_dw_a, ln_g, ln_b, w_conv_b, b_conv_b,
              w_r, b_r, w_i, b_i, lam, w_out, w_pe, w_pg, g_final):
    weights = (g_norm, w_in, w_dw_a, b_dw_a, ln_g, ln_b, w_conv_b, b_conv_b,
               w_r, b_r, w_i, b_i, lam, w_out, w_pe, w_pg, g_final)
    bp = x_prompt.shape[0]
    za = jnp.zeros((DEPTH, bp, CONV_A_WIDTH - 1, W_A), x_prompt.dtype)
    zb = jnp.zeros((DEPTH, bp, CONV_B_WIDTH - 1, W_B), x_prompt.dtype)
    zh = jnp.zeros((DEPTH, bp, W_B), state_h.dtype)
    y_prompt, na_p, nb_p, nh_p = _trunk(x_prompt, p_prompt, za, zb, zh, *weights)
    y_sample, na_s, nb_s, nh_s = _trunk(x_sample, p_sample, state_conv_a, state_conv_b, state_h, *weights)
    return (y_prompt, y_sample, na_p, nb_p, nh_p, na_s, nb_s, nh_s)
```

```python
import functools

import jax
import jax.numpy as jnp
from jax import lax
from jax.experimental import pallas as pl
from jax.experimental.pallas import tpu as pltpu

D_MODEL = 1024
W_A = D_MODEL
W_B = D_MODEL
N_B_HEADS = 8
HEAD_B = W_B // N_B_HEADS
W_IN = 3 * W_A + 2 * W_B
CONV_A_WIDTH = 31
CONV_B_WIDTH = 4
RG_C = 8.0
PLE_DIM = 256
EPS = 1e-6

SUBLANES = 8
LANES = 128
VMEM_LIMIT_BYTES = 60000 * 1024

TT = 64
HIST_A = 32
HIST_B = 8
ROW_TILE = 32
CONV_T = 8
SAMPLE_BT = 16

F32 = jnp.float32
BF16 = jnp.bfloat16


def _sigmoid(x):
    return jax.nn.sigmoid(x)


def _silu(x):
    return x * jax.nn.sigmoid(x)


def _dot(a, b):
    return jnp.dot(a, b, preferred_element_type=F32)


def _rmsnorm(x, g):
    ms = jnp.mean(x * x, axis=-1, keepdims=True)
    return x * lax.rsqrt(ms + EPS) * g


def _layernorm(x, g, b):
    mu = jnp.mean(x, axis=-1, keepdims=True)
    xc = x - mu
    var = jnp.mean(xc * xc, axis=-1, keepdims=True)
    return xc * lax.rsqrt(var + EPS) * g + b


def _rglru_coeffs(r_pre, i_pre, xb, c_lam):
    r = _sigmoid(r_pre)
    ig = _sigmoid(i_pre)
    log_a = c_lam * r
    a = jnp.exp(log_a)
    th = jnp.tanh(log_a)
    bx = jnp.sqrt(-2.0 * th / (1.0 - th)) * (ig * xb)
    return a, bx


def _prompt_kernel(nb, x_ref, p_ref, gn_ref, win_ref, wdw_ref, bdw_ref, lng_ref, lnb_ref,
                   wcb_ref, bcb_ref, wri_ref, br_ref, bi_ref, lam_ref, wout_ref, wpe_ref,
                   wpg_ref, gf_ref,
                   y_ref, na_ref, nbuf_ref, nh_ref,
                   u_s, z_s, vp_s, cv_s, bxp_s, a_s, b_s, mix_s, h_s, wb_s):
    rows = TT * nb
    ha = HIST_A * nb
    hb = HIST_B * nb
    step = pl.program_id(0)
    n_tiles = rows // ROW_TILE

    @pl.when(step == 0)
    def _init():
        vp_s[0:ha, :] = jnp.zeros((ha, W_A), F32)
        bxp_s[0:hb, :] = jnp.zeros((hb, W_B), F32)
        h_s[...] = jnp.zeros_like(h_s)
        for k in range(CONV_A_WIDTH):
            wb_s[k * nb:(k + 1) * nb, :] = jnp.broadcast_to(wdw_ref[k:k + 1, :], (nb, W_A))

    @pl.when(step > 0)
    def _carry():
        vp_s[0:ha, :] = vp_s[rows:rows + ha, :]
        bxp_s[0:hb, :] = bxp_s[rows:rows + hb, :]

    def tiles(body):
        def wrapped(i, c):
            body(pl.multiple_of(i * ROW_TILE, ROW_TILE))
            return c
        lax.fori_loop(0, n_tiles, wrapped, 0)

    def norm_in(r0):
        x = x_ref[pl.ds(r0, ROW_TILE), :]
        u_s[pl.ds(r0, ROW_TILE), :] = _rmsnorm(x, gn_ref[...]).astype(BF16)
    tiles(norm_in)

    z_s[...] = _dot(u_s[...], win_ref[:, 0:2 * W_A])

    def glu(r0):
        z = z_s[pl.ds(r0, ROW_TILE), :]
        vp_s[pl.ds(ha + r0, ROW_TILE), :] = z[:, 0:W_A] * _sigmoid(z[:, W_A:2 * W_A])
    tiles(glu)

    z_s[:, 0:W_A] = _dot(u_s[...], win_ref[:, 2 * W_A:3 * W_A])

    def conv_a(it, c):
        base = pl.multiple_of(it * (CONV_T * nb), CONV_T * nb)
        for j in range(W_A // LANES):
            ls = slice(j * LANES, (j + 1) * LANES)
            bias = jnp.broadcast_to(bdw_ref[:, ls], (nb, LANES))
            acc = [bias] * CONV_T
            taps = [wb_s[k * nb:(k + 1) * nb, ls] for k in range(CONV_A_WIDTH)]
            for d in range(CONV_T + CONV_A_WIDTH - 1):
                off = (d + HIST_A - (CONV_A_WIDTH - 1)) * nb
                xd = vp_s[pl.ds(base + off, nb), ls]
                for s in range(CONV_T):
                    k = d - s
                    if 0 <= k < CONV_A_WIDTH:
                        acc[s] = acc[s] + taps[k] * xd
            for s in range(CONV_T):
                cv_s[pl.ds(base + s * nb, nb), ls] = acc[s]
        return c
    lax.fori_loop(0, TT // CONV_T, conv_a, 0)

    def mix_a(r0):
        v = _layernorm(cv_s[pl.ds(r0, ROW_TILE), :], lng_ref[...], lnb_ref[...])
        gate = z_s[pl.ds(r0, ROW_TILE), 0:W_A]
        mix_s[pl.ds(r0, ROW_TILE), 0:W_A] = (_silu(v) * _silu(gate)).astype(BF16)
    tiles(mix_a)

    z_s[:, 0:W_B] = _dot(u_s[...], win_ref[:, 3 * W_A:3 * W_A + W_B])

    def stash_bx(r0):
        bxp_s[pl.ds(hb + r0, ROW_TILE), :] = z_s[pl.ds(r0, ROW_TILE), 0:W_B]
    tiles(stash_bx)

    def conv_b(r0):
        acc = jnp.broadcast_to(bcb_ref[...], (ROW_TILE, W_B))
        for k in range(CONV_B_WIDTH):
            off = (k + HIST_B - (CONV_B_WIDTH - 1)) * nb
            acc = acc + wcb_ref[k:k + 1, :] * bxp_s[pl.ds(r0 + off, ROW_TILE), :]
        cv_s[pl.ds(r0, ROW_TILE), :] = acc
        mix_s[pl.ds(r0, ROW_TILE), W_A:W_A + W_B] = acc.astype(BF16)
    tiles(conv_b)

    for h in range(N_B_HEADS):
        z_s[:, 2 * h * HEAD_B:2 * (h + 1) * HEAD_B] = _dot(
            mix_s[:, W_A + h * HEAD_B:W_A + (h + 1) * HEAD_B], wri_ref[h])

    c_lam = -RG_C * jax.nn.softplus(-lam_ref[...])

    def coeffs(r0):
        for h in range(N_B_HEADS):
            ls = slice(h * HEAD_B, (h + 1) * HEAD_B)
            r_pre = z_s[pl.ds(r0, ROW_TILE), 2 * h * HEAD_B:(2 * h + 1) * HEAD_B] + br_ref[:, ls]
            i_pre = z_s[pl.ds(r0, ROW_TILE), (2 * h + 1) * HEAD_B:(2 * h + 2) * HEAD_B] + bi_ref[:, ls]
            a, bx = _rglru_coeffs(r_pre, i_pre, cv_s[pl.ds(r0, ROW_TILE), ls], c_lam[:, ls])
            a_s[pl.ds(r0, ROW_TILE), ls] = a
            b_s[pl.ds(r0, ROW_TILE), ls] = bx
    tiles(coeffs)

    def scan(t, h):
        r0 = pl.multiple_of(t * nb, nb)
        h = a_s[pl.ds(r0, nb), :] * h + b_s[pl.ds(r0, nb), :]
        b_s[pl.ds(r0, nb), :] = h
        return h
    h_s[...] = lax.fori_loop(0, TT, scan, h_s[...], unroll=8)

    z_s[:, 0:W_B] = _dot(u_s[...], win_ref[:, 3 * W_A + W_B:W_IN])

    def mix_b(r0):
        yb = b_s[pl.ds(r0, ROW_TILE), :] * _silu(z_s[pl.ds(r0, ROW_TILE), 0:W_B])
        mix_s[pl.ds(r0, ROW_TILE), W_A:W_A + W_B] = yb.astype(BF16)
    tiles(mix_b)

    z_s[:, 0:D_MODEL] = _dot(mix_s[...], wout_ref[...])

    def resid(r0):
        h1 = x_ref[pl.ds(r0, ROW_TILE), :] + z_s[pl.ds(r0, ROW_TILE), 0:D_MODEL]
        cv_s[pl.ds(r0, ROW_TILE), :] = h1
        u_s[pl.ds(r0, ROW_TILE), :] = h1.astype(BF16)
    tiles(resid)

    z_s[:, 0:D_MODEL] = _dot(u_s[...], wpg_ref[...])
    z_s[:, D_MODEL:2 * D_MODEL] = _dot(p_ref[...].astype(BF16), wpe_ref[...])

    def finish(r0):
        z = z_s[pl.ds(r0, ROW_TILE), :]
        h2 = cv_s[pl.ds(r0, ROW_TILE), :] + z[:, D_MODEL:2 * D_MODEL] * _sigmoid(z[:, 0:D_MODEL])
        y_ref[pl.ds(r0, ROW_TILE), :] = _rmsnorm(h2, gf_ref[...])
    tiles(finish)

    @pl.when(step == pl.num_programs(0) - 1)
    def _state_out():
        na = (CONV_A_WIDTH - 1) * nb
        nbb = (CONV_B_WIDTH - 1) * nb
        na_ref[...] = vp_s[ha + rows - na:ha + rows, :]
        nbuf_ref[...] = bxp_s[hb + rows - nbb:hb + rows, :]
        nh_ref[...] = h_s[...]


def _sample_kernel(x_ref, p_ref, sa_ref, sb_ref, h0_ref, gn_ref, win_ref, wdw_ref, bdw_ref,
                   lng_ref, lnb_ref, wcb_ref, bcb_ref, wri_ref, br_ref, bi_ref, lam_ref,
                   wout_ref, wpe_ref, wpg_ref, gf_ref,
                   y_ref, na_ref, nbuf_ref, nh_ref,
                   u_s, z_s, v_s, cvp_s):
    step = pl.program_id(0)
    ka = CONV_A_WIDTH - 1
    kb = CONV_B_WIDTH - 1

    @pl.when(step == 0)
    def _project():
        u = _rmsnorm(x_ref[...], gn_ref[...]).astype(BF16)
        u_s[...] = u
        z_s[...] = _dot(u, win_ref[...])
        v_s[...] = z_s[:, 0:W_A] * _sigmoid(z_s[:, W_A:2 * W_A])

    w_hist = wdw_ref[0:ka, :]

    def per_seq(b, c):
        row = step * SAMPLE_BT + b
        st = sa_ref[b]
        cvp_s[pl.ds(row, 1), :] = jnp.sum(st * w_hist, axis=0, keepdims=True)
        na_ref[b, 0:ka - 1, :] = sa_ref[b, 1:ka, :]
        na_ref[b, ka - 1:ka, :] = v_s[pl.ds(row, 1), :]
        return c
    lax.fori_loop(0, SAMPLE_BT, per_seq, 0)

    @pl.when(step == pl.num_programs(0) - 1)
    def _rest():
        v = v_s[...]
        cv = cvp_s[...] + wdw_ref[ka:ka + 1, :] * v + bdw_ref[...]
        v = _silu(_layernorm(cv, lng_ref[...], lnb_ref[...])) * _silu(z_s[:, 2 * W_A:3 * W_A])
        mix_a = v.astype(BF16)

        b_x = z_s[:, 3 * W_A:3 * W_A + W_B]
        xb = wcb_ref[kb:kb + 1, :] * b_x + bcb_ref[...]
        for k in range(kb):
            xb = xb + wcb_ref[k:k + 1, :] * sb_ref[k]
        for k in range(kb - 1):
            nbuf_ref[k] = sb_ref[k + 1]
        nbuf_ref[kb - 1] = b_x

        xb16 = xb.astype(BF16)
        c_lam = -RG_C * jax.nn.softplus(-lam_ref[...])
        a_parts, b_parts = [], []
        for h in range(N_B_HEADS):
            ls = slice(h * HEAD_B, (h + 1) * HEAD_B)
            g = _dot(xb16[:, ls], wri_ref[h])
            a, bx = _rglru_coeffs(g[:, 0:HEAD_B] + br_ref[:, ls], g[:, HEAD_B:2 * HEAD_B] + bi_ref[:, ls],
                                  xb[:, ls], c_lam[:, ls])
            a_parts.append(a)
            b_parts.append(bx)
        hn = jnp.concatenate(a_parts, axis=-1) * h0_ref[...] + jnp.concatenate(b_parts, axis=-1)
        nh_ref[...] = hn
        mix_b = (hn * _silu(z_s[:, 3 * W_A + W_B:W_IN])).astype(BF16)

        mix = _dot(mix_a, wout_ref[0:W_A, :]) + _dot(mix_b, wout_ref[W_A:W_A + W_B, :])
        h1 = x_ref[...] + mix
        pe = _dot(p_ref[...].astype(BF16), wpe_ref[...]) * _sigmoid(_dot(h1.astype(BF16), wpg_ref[...]))
        y_ref[...] = _rmsnorm(h1 + pe, gf_ref[...])


def _const_spec(shape):
    zeros = (0,) * len(shape)
    return pl.BlockSpec(shape, lambda i: zeros, pipeline_mode=pl.Buffered(1))


def _weight_specs(nb_heads):
    return [
        _const_spec((1, D_MODEL)),
        _const_spec((D_MODEL, W_IN)),
        _const_spec((CONV_A_WIDTH, W_A)),
        _const_spec((1, W_A)),
        _const_spec((1, W_A)),
        _const_spec((1, W_A)),
        _const_spec((CONV_B_WIDTH, W_B)),
        _const_spec((1, W_B)),
        _const_spec((nb_heads, HEAD_B, 2 * HEAD_B)),
        _const_spec((1, W_B)),
        _const_spec((1, W_B)),
        _const_spec((1, W_B)),
        _const_spec((W_A + W_B, D_MODEL)),
        _const_spec((PLE_DIM, D_MODEL)),
        _const_spec((D_MODEL, D_MODEL)),
        _const_spec((1, D_MODEL)),
    ]


def _prompt_call(x_t, p_t, weights, nb, seq):
    rows = TT * nb
    n_a = (CONV_A_WIDTH - 1) * nb
    n_b = (CONV_B_WIDTH - 1) * nb
    out_shape = (
        jax.ShapeDtypeStruct((seq * nb, D_MODEL), F32),
        jax.ShapeDtypeStruct((n_a, W_A), F32),
        jax.ShapeDtypeStruct((n_b, W_B), F32),
        jax.ShapeDtypeStruct((nb, W_B), F32),
    )
    const_out = lambda shape: pl.BlockSpec(shape, lambda i: (0, 0))
    return pl.pallas_call(
        functools.partial(_prompt_kernel, nb),
        grid=(seq // TT,),
        in_specs=[pl.BlockSpec((rows, D_MODEL), lambda i: (i, 0)),
                  pl.BlockSpec((rows, PLE_DIM), lambda i: (i, 0))] + _weight_specs(N_B_HEADS),
        out_specs=(pl.BlockSpec((rows, D_MODEL), lambda i: (i, 0)),
                   const_out((n_a, W_A)), const_out((n_b, W_B)), const_out((nb, W_B))),
        out_shape=out_shape,
        scratch_shapes=[
            pltpu.VMEM((rows, D_MODEL), BF16),
            pltpu.VMEM((rows, 2 * D_MODEL), F32),
            pltpu.VMEM(((HIST_A + TT) * nb, W_A), F32),
            pltpu.VMEM((rows, D_MODEL), F32),
            pltpu.VMEM(((HIST_B + TT) * nb, W_B), F32),
            pltpu.VMEM((rows, W_B), F32),
            pltpu.VMEM((rows, W_B), F32),
            pltpu.VMEM((rows, W_A + W_B), BF16),
            pltpu.VMEM((nb, W_B), F32),
            pltpu.VMEM((CONV_A_WIDTH * nb, W_A), F32),
        ],
        compiler_params=pltpu.CompilerParams(
            dimension_semantics=("arbitrary",), vmem_limit_bytes=VMEM_LIMIT_BYTES),
        name="prompt_layer",
    )(x_t, p_t, *weights)


def _sample_call(x_s, p_s, sa, sb_t, h0, weights):
    n = x_s.shape[0]
    ka = CONV_A_WIDTH - 1
    kb = CONV_B_WIDTH - 1
    out_shape = (
        jax.ShapeDtypeStruct((n, D_MODEL), F32),
        jax.ShapeDtypeStruct((n, ka, W_A), F32),
        jax.ShapeDtypeStruct((kb, n, W_B), F32),
        jax.ShapeDtypeStruct((n, W_B), F32),
    )
    full2 = lambda shape: pl.BlockSpec(shape, lambda i: (0, 0))
    full3 = lambda shape: pl.BlockSpec(shape, lambda i: (0, 0, 0))
    return pl.pallas_call(
        _sample_kernel,
        grid=(n // SAMPLE_BT,),
        in_specs=[full2((n, D_MODEL)), full2((n, PLE_DIM)),
                  pl.BlockSpec((SAMPLE_BT, ka, W_A), lambda i: (i, 0, 0)),
                  full3((kb, n, W_B)), full2((n, W_B))] + _weight_specs(N_B_HEADS),
        out_specs=(full2((n, D_MODEL)),
                   pl.BlockSpec((SAMPLE_BT, ka, W_A), lambda i: (i, 0, 0)),
                   full3((kb, n, W_B)), full2((n, W_B))),
        out_shape=out_shape,
        scratch_shapes=[
            pltpu.VMEM((n, D_MODEL), BF16),
            pltpu.VMEM((n, W_IN), F32),
            pltpu.VMEM((n, W_A), F32),
            pltpu.VMEM((n, W_A), F32),
        ],
        compiler_params=pltpu.CompilerParams(
            dimension_semantics=("arbitrary",), vmem_limit_bytes=VMEM_LIMIT_BYTES),
        name="sample_layer",
    )(x_s, p_s, sa, sb_t, h0, *weights)


def kernel(x_prompt, x_sample, p_prompt, p_sample, state_conv_a, state_conv_b, state_h, g_norm, w_in, w_dw_a, b_dw_a, ln_g, ln_b, w_conv_b, b_conv_b, w_r, b_r, w_i, b_i, lam, w_out, w_pe, w_pg, g_final):
    depth = g_norm.shape[0]
    assert depth == 1, "single-layer step"
    nb, seq, _ = x_prompt.shape
    assert nb == SUBLANES and seq % TT == 0 and TT >= HIST_A
    row = lambda a: a.reshape(1, -1)
    weights = (
        row(g_norm[0]), w_in[0].astype(BF16), w_dw_a[0], row(b_dw_a[0]), row(ln_g[0]), row(ln_b[0]),
        w_conv_b[0], row(b_conv_b[0]),
        jnp.concatenate([w_r[0], w_i[0]], axis=-1).astype(BF16), row(b_r[0]), row(b_i[0]), row(lam[0]),
        w_out[0].astype(BF16), w_pe[0].astype(BF16), w_pg[0].astype(BF16), row(g_final),
    )

    x_t = jnp.swapaxes(x_prompt, 0, 1).reshape(seq * nb, D_MODEL)
    p_t = jnp.swapaxes(p_prompt[0], 0, 1).reshape(seq * nb, PLE_DIM)
    y_t, na_t, nbuf_t, nh_p = _prompt_call(x_t, p_t, weights, nb, seq)
    untime = lambda a, k: jnp.swapaxes(a.reshape(k, nb, a.shape[-1]), 0, 1)
    y_prompt = untime(y_t, seq)
    na_p = untime(na_t, CONV_A_WIDTH - 1)[None]
    nb_p = untime(nbuf_t, CONV_B_WIDTH - 1)[None]

    n_s = x_sample.shape[0]
    assert x_sample.shape[1] == 1 and n_s % SAMPLE_BT == 0
    y_s, na_s, nbuf_s, nh_s = _sample_call(
        x_sample.reshape(n_s, D_MODEL), p_sample[0].reshape(n_s, PLE_DIM), state_conv_a[0],
        jnp.swapaxes(state_conv_b[0], 0, 1), state_h[0], weights)
    return (y_prompt, y_s.reshape(n_s, 1, D_MODEL), na_p, nb_p, nh_p[None],
            na_s[None], jnp.swapaxes(nbuf_s, 0, 1)[None], nh_s[None])
```

```python
import functools

import jax
import jax.numpy as jnp
from jax import lax
from jax.experimental import pallas as pl
from jax.experimental.pallas import tpu as pltpu

D_MODEL = 1024
W_A = D_MODEL
W_B = D_MODEL
N_B_HEADS = 8
HEAD_B = W_B // N_B_HEADS
W_IN = 3 * W_A + 2 * W_B
CONV_A_WIDTH = 31
CONV_B_WIDTH = 4
RG_C = 8.0
PLE_DIM = 256
EPS = 1e-6

SUBLANES = 8
LANES = 128
MXU_N = 256
VMEM_LIMIT_BYTES = 60000 * 1024

TT = 64
HIST_A = 32
HIST_B = 8
ROW_TILE = 16
CONV_T = 8
SAMPLE_BT = 16

NC_D = D_MODEL // MXU_N
NC_GLU = 2 * W_A // MXU_N
NC_REST = (W_IN - 2 * W_A) // MXU_N
CH_GATE_A = 0
CH_BX = W_A // MXU_N
CH_GATE_B = (W_A + W_B) // MXU_N

F32 = jnp.float32
BF16 = jnp.bfloat16


def _sigmoid(x):
    return 0.5 * jnp.tanh(0.5 * x) + 0.5


def _silu(x):
    hx = 0.5 * x
    return hx * jnp.tanh(hx) + hx


def _dot(a, b):
    return jnp.dot(a, b, preferred_element_type=F32)


def _rmsnorm(x, g):
    ms = jnp.mean(x * x, axis=-1, keepdims=True)
    return x * lax.rsqrt(ms + EPS) * g


def _layernorm(x, g, b):
    mu = jnp.mean(x, axis=-1, keepdims=True)
    xc = x - mu
    var = jnp.mean(xc * xc, axis=-1, keepdims=True)
    return xc * lax.rsqrt(var + EPS) * g + b


def _rglru_coeffs(r_pre, i_pre, xb, c_lam):
    r = _sigmoid(r_pre)
    ig = _sigmoid(i_pre)
    log_a = c_lam * r
    a = jnp.exp(log_a)
    th = jnp.tanh(log_a)
    bx = jnp.sqrt(-2.0 * th / (1.0 - th)) * (ig * xb)
    return a, bx


def _lanes_of_chunks(ref, first, rows_sl):
    return jnp.concatenate([ref[first + c, rows_sl, :] for c in range(NC_D)], axis=-1)


def _prompt_kernel(nb, x_ref, p_ref, gn_ref, wina_ref, winb_ref, wdw_ref, bdw_ref, lng_ref, lnb_ref,
                   wcb_ref, bcb_ref, wri_ref, br_ref, bi_ref, lam_ref, wouta_ref, woutb_ref, wpe_ref,
                   wpg_ref, gf_ref,
                   y_ref, na_ref, nbuf_ref, nh_ref,
                   u_s, z_s, zb_s, vp_s, cv_s, a_s, b_s, mixa_s, mixb_s, xb16_s, h_s, wb_s):
    rows = TT * nb
    ha = HIST_A * nb
    hb = HIST_B * nb
    step = pl.program_id(0)
    n_tiles = rows // ROW_TILE
    cur = slice(hb, hb + rows)

    @pl.when(step == 0)
    def _init():
        vp_s[0:ha, :] = jnp.zeros((ha, W_A), F32)
        for c in range(NC_D):
            zb_s[CH_BX + c, 0:hb, :] = jnp.zeros((hb, MXU_N), F32)
        h_s[...] = jnp.zeros_like(h_s)
        for k in range(CONV_A_WIDTH):
            wb_s[k * nb:(k + 1) * nb, :] = jnp.broadcast_to(wdw_ref[k:k + 1, :], (nb, W_A))

    @pl.when(step > 0)
    def _carry():
        vp_s[0:ha, :] = vp_s[rows:rows + ha, :]
        for c in range(NC_D):
            zb_s[CH_BX + c, 0:hb, :] = zb_s[CH_BX + c, rows:rows + hb, :]

    def tiled_loop(n_iter, tile_body, host_body=None):
        per_iter = n_tiles // n_iter

        def body(it, c):
            if host_body is not None:
                host_body(it)
            for q in range(per_iter):
                tile_body(pl.multiple_of((it * per_iter + q) * ROW_TILE, ROW_TILE))
            return c
        lax.fori_loop(0, n_iter, body, 0)

    def norm_in(r0):
        rs = pl.ds(r0, ROW_TILE)
        u_s[rs, :] = _rmsnorm(x_ref[rs, :], gn_ref[...]).astype(BF16)
    tiled_loop(4, norm_in)

    for c in range(NC_GLU):
        z_s[c] = _dot(u_s[...], wina_ref[c])

    def glu(r0):
        rs = pl.ds(r0, ROW_TILE)
        val = _lanes_of_chunks(z_s, 0, rs)
        gl = _lanes_of_chunks(z_s, NC_D, rs)
        vp_s[pl.ds(ha + r0, ROW_TILE), :] = val * _sigmoid(gl)
    tiled_loop(8, glu)

    conv_iters = NC_REST // 3
    conv_blocks = TT // CONV_T // conv_iters

    def conv_a(it, carry):
        for blk in range(conv_blocks):
            base = pl.multiple_of((it * conv_blocks + blk) * (CONV_T * nb), CONV_T * nb)
            for j in range(W_A // LANES):
                ls = slice(j * LANES, (j + 1) * LANES)
                bias = jnp.broadcast_to(bdw_ref[:, ls], (nb, LANES))
                acc = [bias] * CONV_T
                taps = [wb_s[k * nb:(k + 1) * nb, ls] for k in range(CONV_A_WIDTH)]
                for d in range(CONV_T + CONV_A_WIDTH - 1):
                    off = (d + HIST_A - (CONV_A_WIDTH - 1)) * nb
                    xd = vp_s[pl.ds(base + off, nb), ls]
                    for s in range(CONV_T):
                        k = d - s
                        if 0 <= k < CONV_A_WIDTH:
                            acc[s] = acc[s] + taps[k] * xd
                for s in range(CONV_T):
                    cv_s[pl.ds(base + s * nb, nb), ls] = acc[s]
        for q in range(3):
            ch = it * 3 + q
            zb_s[ch, cur, :] = _dot(u_s[...], winb_ref[ch])
        return carry
    lax.fori_loop(0, conv_iters, conv_a, 0)

    def conv_b(r0):
        rs = pl.ds(r0, ROW_TILE)
        for h in range(N_B_HEADS):
            ls = slice(h * HEAD_B, (h + 1) * HEAD_B)
            ch = CH_BX + (h * HEAD_B) // MXU_N
            lo = (h * HEAD_B) % MXU_N
            acc = jnp.broadcast_to(bcb_ref[:, ls], (ROW_TILE, HEAD_B))
            for k in range(CONV_B_WIDTH):
                off = (k + HIST_B - (CONV_B_WIDTH - 1)) * nb
                acc = acc + wcb_ref[k:k + 1, ls] * zb_s[ch, pl.ds(r0 + off, ROW_TILE), lo:lo + HEAD_B]
            a_s[rs, ls] = acc
            xb16_s[h, rs, :] = acc.astype(BF16)
    tiled_loop(8, conv_b)

    def gate_dot(it):
        for q in range(2):
            h = it * 2 + q
            z_s[h] = _dot(xb16_s[h], wri_ref[h])

    def mix_a(r0):
        rs = pl.ds(r0, ROW_TILE)
        v = _layernorm(cv_s[rs, :], lng_ref[...], lnb_ref[...])
        gate = _lanes_of_chunks(zb_s, CH_GATE_A, pl.ds(hb + r0, ROW_TILE))
        mixa_s[rs, :] = (_silu(v) * _silu(gate)).astype(BF16)
    tiled_loop(N_B_HEADS // 2, mix_a, gate_dot)

    c_lam = -RG_C * jax.nn.softplus(-lam_ref[...])

    def out_a_dot(c):
        zb_s[CH_GATE_A + c, cur, :] = _dot(mixa_s[...], wouta_ref[c])

    def coeffs(r0):
        rs = pl.ds(r0, ROW_TILE)
        for h in range(N_B_HEADS):
            ls = slice(h * HEAD_B, (h + 1) * HEAD_B)
            r_pre = z_s[h, rs, 0:HEAD_B] + br_ref[:, ls]
            i_pre = z_s[h, rs, HEAD_B:2 * HEAD_B] + bi_ref[:, ls]
            a, bx = _rglru_coeffs(r_pre, i_pre, a_s[rs, ls], c_lam[:, ls])
            a_s[rs, ls] = a
            b_s[rs, ls] = bx
    tiled_loop(NC_D, coeffs, out_a_dot)

    def scan(t, h):
        r0 = pl.multiple_of(t * nb, nb)
        h = a_s[pl.ds(r0, nb), :] * h + b_s[pl.ds(r0, nb), :]
        b_s[pl.ds(r0, nb), :] = h
        return h
    h_s[...] = lax.fori_loop(0, TT, scan, h_s[...], unroll=8)

    def mix_b(r0):
        rs = pl.ds(r0, ROW_TILE)
        gate = _lanes_of_chunks(zb_s, CH_GATE_B, pl.ds(hb + r0, ROW_TILE))
        mixb_s[rs, :] = (b_s[rs, :] * _silu(gate)).astype(BF16)
    tiled_loop(8, mix_b)

    for c in range(NC_D):
        z_s[c] = _dot(mixb_s[...], woutb_ref[c])

    def resid(r0):
        rs = pl.ds(r0, ROW_TILE)
        h1 = (x_ref[rs, :] + _lanes_of_chunks(zb_s, CH_GATE_A, pl.ds(hb + r0, ROW_TILE))
              + _lanes_of_chunks(z_s, 0, rs))
        cv_s[rs, :] = h1
        u_s[rs, :] = h1.astype(BF16)
    tiled_loop(8, resid)

    p16 = p_ref[...].astype(BF16)
    for c in range(NC_D):
        z_s[c] = _dot(u_s[...], wpg_ref[c])
        z_s[NC_D + c] = _dot(p16, wpe_ref[c])

    def finish(r0):
        rs = pl.ds(r0, ROW_TILE)
        pg = _lanes_of_chunks(z_s, 0, rs)
        pe = _lanes_of_chunks(z_s, NC_D, rs)
        y_ref[rs, :] = _rmsnorm(cv_s[rs, :] + pe * _sigmoid(pg), gf_ref[...])
    tiled_loop(4, finish)

    @pl.when(step == pl.num_programs(0) - 1)
    def _state_out():
        na = (CONV_A_WIDTH - 1) * nb
        nbb = (CONV_B_WIDTH - 1) * nb
        na_ref[...] = vp_s[ha + rows - na:ha + rows, :]
        for c in range(NC_D):
            nbuf_ref[:, c * MXU_N:(c + 1) * MXU_N] = zb_s[CH_BX + c, hb + rows - nbb:hb + rows, :]
        nh_ref[...] = h_s[...]


def _sample_kernel(x_ref, p_ref, sa_ref, sb_ref, h0_ref, gn_ref, wina_ref, winb_ref, wdw_ref, bdw_ref,
                   lng_ref, lnb_ref, wcb_ref, bcb_ref, wri_ref, br_ref, bi_ref, lam_ref,
                   wouta_ref, woutb_ref, wpe_ref, wpg_ref, gf_ref,
                   y_ref, na_ref, nbuf_ref, nh_ref,
                   u_s, z_s, v_s, cvp_s):
    step = pl.program_id(0)
    ka = CONV_A_WIDTH - 1
    kb = CONV_B_WIDTH - 1

    def chunked_dot(lhs, w_ref, first, count):
        return jnp.concatenate([_dot(lhs, w_ref[first + c]) for c in range(count)], axis=-1)

    @pl.when(step == 0)
    def _project():
        u = _rmsnorm(x_ref[...], gn_ref[...]).astype(BF16)
        u_s[...] = u
        za = chunked_dot(u, wina_ref, 0, NC_GLU)
        v_s[...] = za[:, 0:W_A] * _sigmoid(za[:, W_A:2 * W_A])
        z_s[...] = chunked_dot(u, winb_ref, 0, NC_REST)

    w_hist = wdw_ref[0:ka, :]

    def per_seq(b, c):
        row = step * SAMPLE_BT + b
        st = sa_ref[b]
        cvp_s[pl.ds(row, 1), :] = jnp.sum(st * w_hist, axis=0, keepdims=True)
        na_ref[b, 0:ka - 1, :] = sa_ref[b, 1:ka, :]
        na_ref[b, ka - 1:ka, :] = v_s[pl.ds(row, 1), :]
        return c
    lax.fori_loop(0, SAMPLE_BT, per_seq, 0)

    @pl.when(step == pl.num_programs(0) - 1)
    def _rest():
        v = v_s[...]
        cv = cvp_s[...] + wdw_ref[ka:ka + 1, :] * v + bdw_ref[...]
        v = _silu(_layernorm(cv, lng_ref[...], lnb_ref[...])) * _silu(z_s[:, 0:W_A])
        mix_a = v.astype(BF16)

        b_x = z_s[:, W_A:W_A + W_B]
        xb = wcb_ref[kb:kb + 1, :] * b_x + bcb_ref[...]
        for k in range(kb):
            xb = xb + wcb_ref[k:k + 1, :] * sb_ref[k]
        for k in range(kb - 1):
            nbuf_ref[k] = sb_ref[k + 1]
        nbuf_ref[kb - 1] = b_x

        xb16 = xb.astype(BF16)
        c_lam = -RG_C * jax.nn.softplus(-lam_ref[...])
        a_parts, b_parts = [], []
        for h in range(N_B_HEADS):
            ls = slice(h * HEAD_B, (h + 1) * HEAD_B)
            g = _dot(xb16[:, ls], wri_ref[h])
            a, bx = _rglru_coeffs(g[:, 0:HEAD_B] + br_ref[:, ls], g[:, HEAD_B:2 * HEAD_B] + bi_ref[:, ls],
                                  xb[:, ls], c_lam[:, ls])
            a_parts.append(a)
            b_parts.append(bx)
        hn = jnp.concatenate(a_parts, axis=-1) * h0_ref[...] + jnp.concatenate(b_parts, axis=-1)
        nh_ref[...] = hn
        mix_b = (hn * _silu(z_s[:, W_A + W_B:W_A + 2 * W_B])).astype(BF16)

        mix = chunked_dot(mix_a, wouta_ref, 0, NC_D) + chunked_dot(mix_b, woutb_ref, 0, NC_D)
        h1 = x_ref[...] + mix
        pe = (chunked_dot(p_ref[...].astype(BF16), wpe_ref, 0, NC_D)
              * _sigmoid(chunked_dot(h1.astype(BF16), wpg_ref, 0, NC_D)))
        y_ref[...] = _rmsnorm(h1 + pe, gf_ref[...])


def _const_spec(shape):
    zeros = (0,) * len(shape)
    return pl.BlockSpec(shape, lambda i: zeros, pipeline_mode=pl.Buffered(1))


def _weight_specs():
    return [
        _const_spec((1, D_MODEL)),
        _const_spec((NC_GLU, D_MODEL, MXU_N)),
        _const_spec((NC_REST, D_MODEL, MXU_N)),
        _const_spec((CONV_A_WIDTH, W_A)),
        _const_spec((1, W_A)),
        _const_spec((1, W_A)),
        _const_spec((1, W_A)),
        _const_spec((CONV_B_WIDTH, W_B)),
        _const_spec((1, W_B)),
        _const_spec((N_B_HEADS, HEAD_B, 2 * HEAD_B)),
        _const_spec((1, W_B)),
        _const_spec((1, W_B)),
        _const_spec((1, W_B)),
        _const_spec((NC_D, W_A, MXU_N)),
        _const_spec((NC_D, W_B, MXU_N)),
        _const_spec((NC_D, PLE_DIM, MXU_N)),
        _const_spec((NC_D, D_MODEL, MXU_N)),
        _const_spec((1, D_MODEL)),
    ]


def _col_chunks(w):
    k, n = w.shape
    return jnp.swapaxes(w.astype(BF16).reshape(k, n // MXU_N, MXU_N), 0, 1)


def _prompt_call(x_t, p_t, weights, nb, seq):
    rows = TT * nb
    n_a = (CONV_A_WIDTH - 1) * nb
    n_b = (CONV_B_WIDTH - 1) * nb
    out_shape = (
        jax.ShapeDtypeStruct((seq * nb, D_MODEL), F32),
        jax.ShapeDtypeStruct((n_a, W_A), F32),
        jax.ShapeDtypeStruct((n_b, W_B), F32),
        jax.ShapeDtypeStruct((nb, W_B), F32),
    )
    const_out = lambda shape: pl.BlockSpec(shape, lambda i: (0, 0))
    return pl.pallas_call(
        functools.partial(_prompt_kernel, nb),
        grid=(seq // TT,),
        in_specs=[pl.BlockSpec((rows, D_MODEL), lambda i: (i, 0)),
                  pl.BlockSpec((rows, PLE_DIM), lambda i: (i, 0))] + _weight_specs(),
        out_specs=(pl.BlockSpec((rows, D_MODEL), lambda i: (i, 0)),
                   const_out((n_a, W_A)), const_out((n_b, W_B)), const_out((nb, W_B))),
        out_shape=out_shape,
        scratch_shapes=[
            pltpu.VMEM((rows, D_MODEL), BF16),
            pltpu.VMEM((NC_GLU, rows, MXU_N), F32),
            pltpu.VMEM((NC_REST, HIST_B * nb + rows, MXU_N), F32),
            pltpu.VMEM(((HIST_A + TT) * nb, W_A), F32),
            pltpu.VMEM((rows, D_MODEL), F32),
            pltpu.VMEM((rows, W_B), F32),
            pltpu.VMEM((rows, W_B), F32),
            pltpu.VMEM((rows, W_A), BF16),
            pltpu.VMEM((rows, W_B), BF16),
            pltpu.VMEM((N_B_HEADS, rows, HEAD_B), BF16),
            pltpu.VMEM((nb, W_B), F32),
            pltpu.VMEM((CONV_A_WIDTH * nb, W_A), F32),
        ],
        compiler_params=pltpu.CompilerParams(
            dimension_semantics=("arbitrary",), vmem_limit_bytes=VMEM_LIMIT_BYTES),
        name="prompt_layer",
    )(x_t, p_t, *weights)


def _sample_call(x_s, p_s, sa, sb_t, h0, weights):
    n = x_s.shape[0]
    ka = CONV_A_WIDTH - 1
    kb = CONV_B_WIDTH - 1
    out_shape = (
        jax.ShapeDtypeStruct((n, D_MODEL), F32),
        jax.ShapeDtypeStruct((n, ka, W_A), F32),
        jax.ShapeDtypeStruct((kb, n, W_B), F32),
        jax.ShapeDtypeStruct((n, W_B), F32),
    )
    full2 = lambda shape: pl.BlockSpec(shape, lambda i: (0, 0))
    full3 = lambda shape: pl.BlockSpec(shape, lambda i: (0, 0, 0))
    return pl.pallas_call(
        _sample_kernel,
        grid=(n // SAMPLE_BT,),
        in_specs=[full2((n, D_MODEL)), full2((n, PLE_DIM)),
                  pl.BlockSpec((SAMPLE_BT, ka, W_A), lambda i: (i, 0, 0)),
                  full3((kb, n, W_B)), full2((n, W_B))] + _weight_specs(),
        out_specs=(full2((n, D_MODEL)),
                   pl.BlockSpec((SAMPLE_BT, ka, W_A), lambda i: (i, 0, 0)),
                   full3((kb, n, W_B)), full2((n, W_B))),
        out_shape=out_shape,
        scratch_shapes=[
            pltpu.VMEM((n, D_MODEL), BF16),
            pltpu.VMEM((n, W_IN - 2 * W_A), F32),
            pltpu.VMEM((n, W_A), F32),
            pltpu.VMEM((n, W_A), F32),
        ],
        compiler_params=pltpu.CompilerParams(
            dimension_semantics=("arbitrary",), vmem_limit_bytes=VMEM_LIMIT_BYTES),
        name="sample_layer",
    )(x_s, p_s, sa, sb_t, h0, *weights)


def kernel(x_prompt, x_sample, p_prompt, p_sample, state_conv_a, state_conv_b, state_h, g_norm, w_in, w_dw_a, b_dw_a, ln_g, ln_b, w_conv_b, b_conv_b, w_r, b_r, w_i, b_i, lam, w_out, w_pe, w_pg, g_final):
    depth = g_norm.shape[0]
    assert depth == 1, "single-layer step"
    nb, seq, _ = x_prompt.shape
    assert nb == SUBLANES and seq % TT == 0 and TT >= HIST_A
    row = lambda a: a.reshape(1, -1)
    weights = (
        row(g_norm[0]), _col_chunks(w_in[0][:, 0:2 * W_A]), _col_chunks(w_in[0][:, 2 * W_A:]),
        w_dw_a[0], row(b_dw_a[0]), row(ln_g[0]), row(ln_b[0]),
        w_conv_b[0], row(b_conv_b[0]),
        jnp.concatenate([w_r[0], w_i[0]], axis=-1).astype(BF16), row(b_r[0]), row(b_i[0]), row(lam[0]),
        _col_chunks(w_out[0][0:W_A]), _col_chunks(w_out[0][W_A:]), _col_chunks(w_pe[0]),
        _col_chunks(w_pg[0]), row(g_final),
    )

    x_t = jnp.swapaxes(x_prompt, 0, 1).reshape(seq * nb, D_MODEL)
    p_t = jnp.swapaxes(p_prompt[0], 0, 1).reshape(seq * nb, PLE_DIM)
    y_t, na_t, nbuf_t, nh_p = _prompt_call(x_t, p_t, weights, nb, seq)
    untime = lambda a, k: jnp.swapaxes(a.reshape(k, nb, a.shape[-1]), 0, 1)
    y_prompt = untime(y_t, seq)
    na_p = untime(na_t, CONV_A_WIDTH - 1)[None]
    nb_p = untime(nbuf_t, CONV_B_WIDTH - 1)[None]

    n_s = x_sample.shape[0]
    assert x_sample.shape[1] == 1 and n_s % SAMPLE_BT == 0
    y_s, na_s, nbuf_s, nh_s = _sample_call(
        x_sample.reshape(n_s, D_MODEL), p_sample[0].reshape(n_s, PLE_DIM), state_conv_a[0],
        jnp.swapaxes(state_conv_b[0], 0, 1), state_h[0], weights)
    return (y_prompt, y_s.reshape(n_s, 1, D_MODEL), na_p, nb_p, nh_p[None],
            na_s[None], jnp.swapaxes(nbuf_s, 0, 1)[None], nh_s[None])
```

```python
import functools

import jax
import jax.numpy as jnp
from jax import lax
from jax.experimental import pallas as pl
from jax.experimental.pallas import tpu as pltpu

D_MODEL = 1024
W_A = D_MODEL
W_B = D_MODEL
N_B_HEADS = 8
HEAD_B = W_B // N_B_HEADS
W_IN = 3 * W_A + 2 * W_B
CONV_A_WIDTH = 31
CONV_B_WIDTH = 4
RG_C = 8.0
PLE_DIM = 256
EPS = 1e-6

SUBLANES = 8
LANES = 128
MXU_N = 256
VMEM_LIMIT_BYTES = 60000 * 1024

TT = 64
HIST_A = 32
HIST_B = 8
SLAB_PITCH = TT + 8
ROW_TILE = 16
CONV_T = 8
SAMPLE_BT = 16

NC_D = D_MODEL // MXU_N
NC_GLU = 2 * W_A // MXU_N
NC_REST = (W_IN - 2 * W_A) // MXU_N
CH_GATE_A = 0
CH_BX = W_A // MXU_N
CH_GATE_B = (W_A + W_B) // MXU_N

F32 = jnp.float32
BF16 = jnp.bfloat16


def _sigmoid(x):
    return 0.5 * jnp.tanh(0.5 * x) + 0.5


def _silu(x):
    hx = 0.5 * x
    return hx * jnp.tanh(hx) + hx


def _dot(a, b):
    return jnp.dot(a, b, preferred_element_type=F32)


def _rmsnorm(x, g):
    ms = jnp.mean(x * x, axis=-1, keepdims=True)
    return x * lax.rsqrt(ms + EPS) * g


def _layernorm(x, g, b):
    mu = jnp.mean(x, axis=-1, keepdims=True)
    xc = x - mu
    var = jnp.mean(xc * xc, axis=-1, keepdims=True)
    return xc * lax.rsqrt(var + EPS) * g + b


def _rglru_coeffs(r_pre, i_pre, xb, c_lam):
    r = _sigmoid(r_pre)
    ig = _sigmoid(i_pre)
    log_a = c_lam * r
    a = jnp.exp(log_a)
    th = jnp.tanh(log_a)
    bx = jnp.sqrt(-2.0 * th / (1.0 - th)) * (ig * xb)
    return a, bx


def _aligned(v, m):
    return v if isinstance(v, int) else pl.multiple_of(v, m)


def _chunk_cols(c):
    return slice(c * MXU_N, (c + 1) * MXU_N)


def _lanes_of_chunks(ref, first, rows_sl):
    return jnp.concatenate([ref[first + c, rows_sl, :] for c in range(NC_D)], axis=-1)


def _prompt_kernel(nb, x_hbm, p_hbm, gn_ref, win_ref, wdw_ref, bdw_ref, lng_ref, lnb_ref,
                   wcb_ref, bcb_ref, wri_ref, br_ref, bi_ref, lam_ref, wout_ref, wpe_ref,
                   wpg_ref, gf_ref,
                   y_hbm, na_ref, nbuf_ref, nh_ref,
                   u_s, z_s, zb_s, vp_s, cv_s, a_s, b_s, mixa_s, mixb_s, xb16_s, h_s, wb_s,
                   xs_s, ps_s, ys_s, xt_s, p16_s, sem_in, sem_out):
    rows = TT * nb
    ha = HIST_A * nb
    hb = HIST_B * nb
    step = pl.program_id(0)
    n_steps = pl.num_programs(0)
    n_tiles = rows // ROW_TILE
    cur = slice(hb, hb + rows)
    slot = step % 2

    def in_copies(s, sl):
        t0 = pl.multiple_of(s * TT, TT)
        cps = []
        for b in range(nb):
            dst_rows = pl.ds(b * SLAB_PITCH, TT)
            for j in range(D_MODEL // LANES):
                cps.append(pltpu.make_async_copy(
                    x_hbm.at[b, pl.ds(t0, TT), pl.ds(j * LANES, LANES)],
                    xs_s.at[sl, j, dst_rows, :], sem_in.at[sl]))
            for j in range(PLE_DIM // LANES):
                cps.append(pltpu.make_async_copy(
                    p_hbm.at[b, pl.ds(t0, TT), pl.ds(j * LANES, LANES)],
                    ps_s.at[sl, j, dst_rows, :], sem_in.at[sl]))
        return cps

    def out_copies(s):
        t0 = pl.multiple_of(s * TT, TT)
        return [pltpu.make_async_copy(
            ys_s.at[j, pl.ds(b * SLAB_PITCH, TT), :],
            y_hbm.at[b, pl.ds(t0, TT), pl.ds(j * LANES, LANES)], sem_out.at[0])
            for b in range(nb) for j in range(D_MODEL // LANES)]

    @pl.when(step == 0)
    def _first_fetch():
        for cp in in_copies(0, 0):
            cp.start()

    @pl.when(step + 1 < n_steps)
    def _prefetch():
        for cp in in_copies(step + 1, 1 - slot):
            cp.start()

    @pl.when(step == 0)
    def _init():
        vp_s[0:ha, :] = jnp.zeros((ha, W_A), F32)
        for c in range(NC_D):
            zb_s[CH_BX + c, 0:hb, :] = jnp.zeros((hb, MXU_N), F32)
        h_s[...] = jnp.zeros_like(h_s)
        for k in range(CONV_A_WIDTH):
            wb_s[k * nb:(k + 1) * nb, :] = jnp.broadcast_to(wdw_ref[k:k + 1, :], (nb, W_A))

    @pl.when(step > 0)
    def _carry():
        vp_s[0:ha, :] = vp_s[rows:rows + ha, :]
        for c in range(NC_D):
            zb_s[CH_BX + c, 0:hb, :] = zb_s[CH_BX + c, rows:rows + hb, :]

    def tiled_loop(n_iter, tile_body, host_body=None):
        per_iter = n_tiles // n_iter

        def body(it, c):
            if host_body is not None:
                host_body(it)
            for q in range(per_iter):
                tile_body(_aligned((it * per_iter + q) * ROW_TILE, ROW_TILE))
            return c
        if host_body is not None:
            for it in range(n_iter):
                body(it, 0)
        else:
            lax.fori_loop(0, n_iter, body, 0)

    for cp in in_copies(step, slot):
        cp.wait()

    def time_rows(slab, groups, r0):
        t = lax.shift_right_logical(r0, SUBLANES.bit_length() - 1)
        return jnp.concatenate(
            [jnp.concatenate([slab[slot, j, pl.ds(t + q, nb, stride=SLAB_PITCH), :] for j in range(groups)],
                             axis=-1) for q in range(ROW_TILE // nb)], axis=0)

    def norm_in(r0):
        rs = pl.ds(r0, ROW_TILE)
        x = time_rows(xs_s, D_MODEL // LANES, r0)
        xt_s[rs, :] = x
        u_s[rs, :] = _rmsnorm(x, gn_ref[...]).astype(BF16)
        p16_s[rs, :] = time_rows(ps_s, PLE_DIM // LANES, r0).astype(BF16)
    tiled_loop(4, norm_in)

    for c in range(NC_GLU):
        z_s[c] = _dot(u_s[...], win_ref[:, _chunk_cols(c)])

    def glu(r0):
        rs = pl.ds(r0, ROW_TILE)
        val = _lanes_of_chunks(z_s, 0, rs)
        gl = _lanes_of_chunks(z_s, NC_D, rs)
        vp_s[pl.ds(ha + r0, ROW_TILE), :] = val * _sigmoid(gl)
    tiled_loop(8, glu)

    conv_iters = NC_REST // 3
    conv_blocks = TT // CONV_T // conv_iters

    def conv_a(it, carry):
        for blk in range(conv_blocks):
            base = _aligned((it * conv_blocks + blk) * (CONV_T * nb), CONV_T * nb)
            for j in range(W_A // LANES):
                ls = slice(j * LANES, (j + 1) * LANES)
                bias = jnp.broadcast_to(bdw_ref[:, ls], (nb, LANES))
                acc = [bias] * CONV_T
                taps = [wb_s[k * nb:(k + 1) * nb, ls] for k in range(CONV_A_WIDTH)]
                for d in range(CONV_T + CONV_A_WIDTH - 1):
                    off = (d + HIST_A - (CONV_A_WIDTH - 1)) * nb
                    xd = vp_s[pl.ds(base + off, nb), ls]
                    for s in range(CONV_T):
                        k = d - s
                        if 0 <= k < CONV_A_WIDTH:
                            acc[s] = acc[s] + taps[k] * xd
                for s in range(CONV_T):
                    cv_s[pl.ds(base + s * nb, nb), ls] = acc[s]
        for q in range(3):
            ch = it * 3 + q
            zb_s[ch, cur, :] = _dot(u_s[...], win_ref[:, _chunk_cols(NC_GLU + ch)])
        return carry
    for it in range(conv_iters):
        conv_a(it, 0)

    def conv_b(r0):
        rs = pl.ds(r0, ROW_TILE)
        for h in range(N_B_HEADS):
            ls = slice(h * HEAD_B, (h + 1) * HEAD_B)
            ch = CH_BX + (h * HEAD_B) // MXU_N
            lo = (h * HEAD_B) % MXU_N
            acc = jnp.broadcast_to(bcb_ref[:, ls], (ROW_TILE, HEAD_B))
            for k in range(CONV_B_WIDTH):
                off = (k + HIST_B - (CONV_B_WIDTH - 1)) * nb
                acc = acc + wcb_ref[k:k + 1, ls] * zb_s[ch, pl.ds(r0 + off, ROW_TILE), lo:lo + HEAD_B]
            a_s[rs, ls] = acc
            xb16_s[h, rs, :] = acc.astype(BF16)
    tiled_loop(8, conv_b)

    def gate_dot(it):
        for q in range(2):
            h = it * 2 + q
            z_s[h] = _dot(xb16_s[h], wri_ref[h])

    def mix_a(r0):
        rs = pl.ds(r0, ROW_TILE)
        v = _layernorm(cv_s[rs, :], lng_ref[...], lnb_ref[...])
        gate = _lanes_of_chunks(zb_s, CH_GATE_A, pl.ds(hb + r0, ROW_TILE))
        mixa_s[rs, :] = (_silu(v) * _silu(gate)).astype(BF16)
    tiled_loop(N_B_HEADS // 2, mix_a, gate_dot)

    c_lam = -RG_C * jax.nn.softplus(-lam_ref[...])

    def out_a_dot(c):
        zb_s[CH_GATE_A + c, cur, :] = _dot(mixa_s[...], wout_ref[0:W_A, _chunk_cols(c)])

    def coeffs(r0):
        rs = pl.ds(r0, ROW_TILE)
        for h in range(N_B_HEADS):
            ls = slice(h * HEAD_B, (h + 1) * HEAD_B)
            r_pre = z_s[h, rs, 0:HEAD_B] + br_ref[:, ls]
            i_pre = z_s[h, rs, HEAD_B:2 * HEAD_B] + bi_ref[:, ls]
            a, bx = _rglru_coeffs(r_pre, i_pre, a_s[rs, ls], c_lam[:, ls])
            a_s[rs, ls] = a
            b_s[rs, ls] = bx
    tiled_loop(NC_D, coeffs, out_a_dot)

    def scan(t, h):
        r0 = pl.multiple_of(t * nb, nb)
        h = a_s[pl.ds(r0, nb), :] * h + b_s[pl.ds(r0, nb), :]
        b_s[pl.ds(r0, nb), :] = h
        return h
    h_s[...] = lax.fori_loop(0, TT, scan, h_s[...], unroll=8)

    def mix_b(r0):
        rs = pl.ds(r0, ROW_TILE)
        gate = _lanes_of_chunks(zb_s, CH_GATE_B, pl.ds(hb + r0, ROW_TILE))
        mixb_s[rs, :] = (b_s[rs, :] * _silu(gate)).astype(BF16)
    tiled_loop(8, mix_b)

    for c in range(NC_D):
        z_s[c] = _dot(mixb_s[...], wout_ref[W_A:W_A + W_B, _chunk_cols(c)])

    def resid(r0):
        rs = pl.ds(r0, ROW_TILE)
        h1 = (xt_s[rs, :] + _lanes_of_chunks(zb_s, CH_GATE_A, pl.ds(hb + r0, ROW_TILE))
              + _lanes_of_chunks(z_s, 0, rs))
        cv_s[rs, :] = h1
        u_s[rs, :] = h1.astype(BF16)
    tiled_loop(8, resid)

    for c in range(NC_D):
        z_s[c] = _dot(u_s[...], wpg_ref[:, _chunk_cols(c)])
        z_s[NC_D + c] = _dot(p16_s[...], wpe_ref[:, _chunk_cols(c)])

    @pl.when(step > 0)
    def _drain_prev():
        for cp in out_copies(step - 1):
            cp.wait()

    def finish(r0):
        rs = pl.ds(r0, ROW_TILE)
        pg = _lanes_of_chunks(z_s, 0, rs)
        pe = _lanes_of_chunks(z_s, NC_D, rs)
        y = _rmsnorm(cv_s[rs, :] + pe * _sigmoid(pg), gf_ref[...])
        t = lax.shift_right_logical(r0, SUBLANES.bit_length() - 1)
        for q in range(ROW_TILE // nb):
            for j in range(D_MODEL // LANES):
                ys_s[j, pl.ds(t + q, nb, stride=SLAB_PITCH), :] = y[q * nb:(q + 1) * nb, j * LANES:(j + 1) * LANES]
    tiled_loop(4, finish)

    for cp in out_copies(step):
        cp.start()

    @pl.when(step == n_steps - 1)
    def _drain_last():
        for cp in out_copies(step):
            cp.wait()

    @pl.when(step == n_steps - 1)
    def _state_out():
        na = (CONV_A_WIDTH - 1) * nb
        nbb = (CONV_B_WIDTH - 1) * nb
        na_ref[...] = vp_s[ha + rows - na:ha + rows, :]
        for c in range(NC_D):
            nbuf_ref[:, c * MXU_N:(c + 1) * MXU_N] = zb_s[CH_BX + c, hb + rows - nbb:hb + rows, :]
        nh_ref[...] = h_s[...]


def _sample_kernel(x_ref, p_ref, sa_ref, sb_ref, h0_ref, gn_ref, win_ref, wdw_ref, bdw_ref,
                   lng_ref, lnb_ref, wcb_ref, bcb_ref, wri_ref, br_ref, bi_ref, lam_ref,
                   wout_ref, wpe_ref, wpg_ref, gf_ref,
                   y_ref, na_ref, nbuf_ref, nh_ref,
                   u_s, z_s, v_s, cvp_s):
    step = pl.program_id(0)
    ka = CONV_A_WIDTH - 1
    kb = CONV_B_WIDTH - 1

    @pl.when(step == 0)
    def _project():
        u = _rmsnorm(x_ref[...], gn_ref[...]).astype(BF16)
        u_s[...] = u
        za = _dot(u, win_ref[:, 0:2 * W_A])
        v_s[...] = za[:, 0:W_A] * _sigmoid(za[:, W_A:2 * W_A])
        z_s[...] = _dot(u, win_ref[:, 2 * W_A:W_IN])

    w_hist = wdw_ref[0:ka, :]

    def per_seq(b, c):
        row = step * SAMPLE_BT + b
        st = sa_ref[b]
        cvp_s[pl.ds(row, 1), :] = jnp.sum(st * w_hist, axis=0, keepdims=True)
        na_ref[b, 0:ka - 1, :] = sa_ref[b, 1:ka, :]
        na_ref[b, ka - 1:ka, :] = v_s[pl.ds(row, 1), :]
        return c
    lax.fori_loop(0, SAMPLE_BT, per_seq, 0)

    @pl.when(step == pl.num_programs(0) - 1)
    def _rest():
        v = v_s[...]
        cv = cvp_s[...] + wdw_ref[ka:ka + 1, :] * v + bdw_ref[...]
        v = _silu(_layernorm(cv, lng_ref[...], lnb_ref[...])) * _silu(z_s[:, 0:W_A])
        mix_a = v.astype(BF16)

        b_x = z_s[:, W_A:W_A + W_B]
        xb = wcb_ref[kb:kb + 1, :] * b_x + bcb_ref[...]
        for k in range(kb):
            xb = xb + wcb_ref[k:k + 1, :] * sb_ref[k]
        for k in range(kb - 1):
            nbuf_ref[k] = sb_ref[k + 1]
        nbuf_ref[kb - 1] = b_x

        xb16 = xb.astype(BF16)
        c_lam = -RG_C * jax.nn.softplus(-lam_ref[...])
        a_parts, b_parts = [], []
        for h in range(N_B_HEADS):
            ls = slice(h * HEAD_B, (h + 1) * HEAD_B)
            g = _dot(xb16[:, ls], wri_ref[h])
            a, bx = _rglru_coeffs(g[:, 0:HEAD_B] + br_ref[:, ls], g[:, HEAD_B:2 * HEAD_B] + bi_ref[:, ls],
                                  xb[:, ls], c_lam[:, ls])
            a_parts.append(a)
            b_parts.append(bx)
        hn = jnp.concatenate(a_parts, axis=-1) * h0_ref[...] + jnp.concatenate(b_parts, axis=-1)
        nh_ref[...] = hn
        mix_b = (hn * _silu(z_s[:, W_A + W_B:W_A + 2 * W_B])).astype(BF16)

        mix = _dot(mix_a, wout_ref[0:W_A, :]) + _dot(mix_b, wout_ref[W_A:W_A + W_B, :])
        h1 = x_ref[...] + mix
        pe = _dot(p_ref[...].astype(BF16), wpe_ref[...]) * _sigmoid(_dot(h1.astype(BF16), wpg_ref[...]))
        y_ref[...] = _rmsnorm(h1 + pe, gf_ref[...])


def _const_spec(shape):
    zeros = (0,) * len(shape)
    return pl.BlockSpec(shape, lambda i: zeros, pipeline_mode=pl.Buffered(1))


def _weight_specs():
    return [
        _const_spec((1, D_MODEL)),
        _const_spec((D_MODEL, W_IN)),
        _const_spec((CONV_A_WIDTH, W_A)),
        _const_spec((1, W_A)),
        _const_spec((1, W_A)),
        _const_spec((1, W_A)),
        _const_spec((CONV_B_WIDTH, W_B)),
        _const_spec((1, W_B)),
        _const_spec((N_B_HEADS, HEAD_B, 2 * HEAD_B)),
        _const_spec((1, W_B)),
        _const_spec((1, W_B)),
        _const_spec((1, W_B)),
        _const_spec((W_A + W_B, D_MODEL)),
        _const_spec((PLE_DIM, D_MODEL)),
        _const_spec((D_MODEL, D_MODEL)),
        _const_spec((1, D_MODEL)),
    ]


def _prompt_call(x, p, weights):
    nb, seq, _ = x.shape
    rows = TT * nb
    n_a = (CONV_A_WIDTH - 1) * nb
    n_b = (CONV_B_WIDTH - 1) * nb
    slab_rows = nb * SLAB_PITCH
    hbm = pl.BlockSpec(memory_space=pl.ANY)
    out_shape = (
        jax.ShapeDtypeStruct((nb, seq, D_MODEL), F32),
        jax.ShapeDtypeStruct((n_a, W_A), F32),
        jax.ShapeDtypeStruct((n_b, W_B), F32),
        jax.ShapeDtypeStruct((nb, W_B), F32),
    )
    const_out = lambda shape: pl.BlockSpec(shape, lambda i: (0, 0))
    return pl.pallas_call(
        functools.partial(_prompt_kernel, nb),
        grid=(seq // TT,),
        in_specs=[hbm, hbm] + _weight_specs(),
        out_specs=(hbm, const_out((n_a, W_A)), const_out((n_b, W_B)), const_out((nb, W_B))),
        out_shape=out_shape,
        scratch_shapes=[
            pltpu.VMEM((rows, D_MODEL), BF16),
            pltpu.VMEM((NC_GLU, rows, MXU_N), F32),
            pltpu.VMEM((NC_REST, HIST_B * nb + rows, MXU_N), F32),
            pltpu.VMEM(((HIST_A + TT) * nb, W_A), F32),
            pltpu.VMEM((rows, D_MODEL), F32),
            pltpu.VMEM((rows, W_B), F32),
            pltpu.VMEM((rows, W_B), F32),
            pltpu.VMEM((rows, W_A), BF16),
            pltpu.VMEM((rows, W_B), BF16),
            pltpu.VMEM((N_B_HEADS, rows, HEAD_B), BF16),
            pltpu.VMEM((nb, W_B), F32),
            pltpu.VMEM((CONV_A_WIDTH * nb, W_A), F32),
            pltpu.VMEM((2, D_MODEL // LANES, slab_rows, LANES), F32),
            pltpu.VMEM((2, PLE_DIM // LANES, slab_rows, LANES), F32),
            pltpu.VMEM((D_MODEL // LANES, slab_rows, LANES), F32),
            pltpu.VMEM((rows, D_MODEL), F32),
            pltpu.VMEM((rows, PLE_DIM), BF16),
            pltpu.SemaphoreType.DMA((2,)),
            pltpu.SemaphoreType.DMA((1,)),
        ],
        compiler_params=pltpu.CompilerParams(
            dimension_semantics=("arbitrary",), vmem_limit_bytes=VMEM_LIMIT_BYTES),
        name="prompt_layer",
    )(x, p, *weights)


def _sample_call(x_s, p_s, sa, sb_t, h0, weights):
    n = x_s.shape[0]
    ka = CONV_A_WIDTH - 1
    kb = CONV_B_WIDTH - 1
    out_shape = (
        jax.ShapeDtypeStruct((n, D_MODEL), F32),
        jax.ShapeDtypeStruct((n, ka, W_A), F32),
        jax.ShapeDtypeStruct((kb, n, W_B), F32),
        jax.ShapeDtypeStruct((n, W_B), F32),
    )
    full2 = lambda shape: pl.BlockSpec(shape, lambda i: (0, 0))
    full3 = lambda shape: pl.BlockSpec(shape, lambda i: (0, 0, 0))
    return pl.pallas_call(
        _sample_kernel,
        grid=(n // SAMPLE_BT,),
        in_specs=[full2((n, D_MODEL)), full2((n, PLE_DIM)),
                  pl.BlockSpec((SAMPLE_BT, ka, W_A), lambda i: (i, 0, 0)),
                  full3((kb, n, W_B)), full2((n, W_B))] + _weight_specs(),
        out_specs=(full2((n, D_MODEL)),
                   pl.BlockSpec((SAMPLE_BT, ka, W_A), lambda i: (i, 0, 0)),
                   full3((kb, n, W_B)), full2((n, W_B))),
        out_shape=out_shape,
        scratch_shapes=[
            pltpu.VMEM((n, D_MODEL), BF16),
            pltpu.VMEM((n, W_IN - 2 * W_A), F32),
            pltpu.VMEM((n, W_A), F32),
            pltpu.VMEM((n, W_A), F32),
        ],
        compiler_params=pltpu.CompilerParams(
            dimension_semantics=("arbitrary",), vmem_limit_bytes=VMEM_LIMIT_BYTES),
        name="sample_layer",
    )(x_s, p_s, sa, sb_t, h0, *weights)


def kernel(x_prompt, x_sample, p_prompt, p_sample, state_conv_a, state_conv_b, state_h, g_norm, w_in, w_dw_a, b_dw_a, ln_g, ln_b, w_conv_b, b_conv_b, w_r, b_r, w_i, b_i, lam, w_out, w_pe, w_pg, g_final):
    depth = g_norm.shape[0]
    assert depth == 1, "single-layer step"
    nb, seq, _ = x_prompt.shape
    assert nb == SUBLANES and seq % TT == 0 and TT >= HIST_A
    row = lambda a: a.reshape(1, -1)
    weights = (
        row(g_norm[0]), w_in[0].astype(BF16),
        w_dw_a[0], row(b_dw_a[0]), row(ln_g[0]), row(ln_b[0]),
        w_conv_b[0], row(b_conv_b[0]),
        jnp.concatenate([w_r[0], w_i[0]], axis=-1).astype(BF16), row(b_r[0]), row(b_i[0]), row(lam[0]),
        w_out[0].astype(BF16), w_pe[0].astype(BF16), w_pg[0].astype(BF16), row(g_final),
    )

    y_prompt, na_t, nbuf_t, nh_p = _prompt_call(x_prompt, p_prompt[0], weights)
    untime = lambda a, k: jnp.swapaxes(a.reshape(k, nb, a.shape[-1]), 0, 1)
    na_p = untime(na_t, CONV_A_WIDTH - 1)[None]
    nb_p = untime(nbuf_t, CONV_B_WIDTH - 1)[None]

    n_s = x_sample.shape[0]
    assert x_sample.shape[1] == 1 and n_s % SAMPLE_BT == 0
    y_s, na_s, nbuf_s, nh_s = _sample_call(
        x_sample.reshape(n_s, D_MODEL), p_sample[0].reshape(n_s, PLE_DIM), state_conv_a[0],
        jnp.swapaxes(state_conv_b[0], 0, 1), state_h[0], weights)
    return (y_prompt, y_s.reshape(n_s, 1, D_MODEL), na_p, nb_p, nh_p[None],
            na_s[None], jnp.swapaxes(nbuf_s, 0, 1)[None], nh_s[None])
```

```python
import functools

import jax
import jax.numpy as jnp
from jax import lax
from jax.experimental import pallas as pl
from jax.experimental.pallas import tpu as pltpu

D_MODEL = 1024
W_A = D_MODEL
W_B = D_MODEL
N_B_HEADS = 8
HEAD_B = W_B // N_B_HEADS
W_IN = 3 * W_A + 2 * W_B
CONV_A_WIDTH = 31
CONV_B_WIDTH = 4
RG_C = 8.0
PLE_DIM = 256
EPS = 1e-6

SUBLANES = 8
LANES = 128
MXU_N = 256
VMEM_LIMIT_BYTES = 60000 * 1024

TT = 64
HIST_A = 32
HIST_B = 8
SLAB_PITCH = TT + 8
ROW_TILE = 16
CONV_T = 8
CONV_J = 2
SAMPLE_BT = 16
CAST_STEPS = 8

NC_D = D_MODEL // MXU_N
NC_GLU = 2 * W_A // MXU_N
NC_REST = (W_IN - 2 * W_A) // MXU_N
CH_GATE_A = 0
CH_BX = W_A // MXU_N
CH_GATE_B = (W_A + W_B) // MXU_N

F32 = jnp.float32
BF16 = jnp.bfloat16


def _sigmoid(x):
    return 0.5 * jnp.tanh(0.5 * x) + 0.5


def _silu(x):
    hx = 0.5 * x
    return hx * jnp.tanh(hx) + hx


def _dot(a, b):
    return jnp.dot(a, b, preferred_element_type=F32)


def _rmsnorm(x, g):
    ms = jnp.mean(x * x, axis=-1, keepdims=True)
    return x * lax.rsqrt(ms + EPS) * g


def _layernorm(x, g, b):
    mu = jnp.mean(x, axis=-1, keepdims=True)
    xc = x - mu
    var = jnp.mean(xc * xc, axis=-1, keepdims=True)
    return xc * lax.rsqrt(var + EPS) * g + b


def _rglru_coeffs(r_pre, i_pre, xb, c_lam):
    r = _sigmoid(r_pre)
    ig = _sigmoid(i_pre)
    log_a = c_lam * r
    a = jnp.exp(log_a)
    th = jnp.tanh(log_a)
    bx = jnp.sqrt(-2.0 * th / (1.0 - th)) * (ig * xb)
    return a, bx


def _aligned(v, m):
    return v if isinstance(v, int) else pl.multiple_of(v, m)


def _chunk_cols(c):
    return slice(c * MXU_N, (c + 1) * MXU_N)


def _lanes_of_chunks(ref, first, rows_sl):
    return jnp.concatenate([ref[first + c, rows_sl, :] for c in range(NC_D)], axis=-1)


def _prompt_kernel(nb, x_hbm, p_hbm, gn_ref, win_ref, wdw_ref, bdw_ref, lng_ref, lnb_ref,
                   wcb_ref, bcb_ref, wri_ref, br_ref, bi_ref, lam_ref, wout_ref, wpe_ref,
                   wpg_ref, gf_ref,
                   y_hbm, na_ref, nbuf_ref, nh_ref,
                   u_s, z_s, zb_s, vp_s, cv_s, a_s, b_s, mixa_s, mixb_s, xb16_s, h_s, wb_s,
                   xs_s, ps_s, ys_s, xt_s, p16_s, sem_in, sem_out):
    rows = TT * nb
    ha = HIST_A * nb
    hb = HIST_B * nb
    step = pl.program_id(0)
    n_steps = pl.num_programs(0)
    n_tiles = rows // ROW_TILE
    cur = slice(hb, hb + rows)
    slot = step % 2

    def in_copies(s, sl):
        t0 = pl.multiple_of(s * TT, TT)
        cps = []
        for b in range(nb):
            dst_rows = pl.ds(b * SLAB_PITCH, TT)
            for j in range(D_MODEL // LANES):
                cps.append(pltpu.make_async_copy(
                    x_hbm.at[b, pl.ds(t0, TT), pl.ds(j * LANES, LANES)],
                    xs_s.at[sl, j, dst_rows, :], sem_in.at[sl]))
            for j in range(PLE_DIM // LANES):
                cps.append(pltpu.make_async_copy(
                    p_hbm.at[b, pl.ds(t0, TT), pl.ds(j * LANES, LANES)],
                    ps_s.at[sl, j, dst_rows, :], sem_in.at[sl]))
        return cps

    def out_copies(s):
        t0 = pl.multiple_of(s * TT, TT)
        return [pltpu.make_async_copy(
            ys_s.at[j, pl.ds(b * SLAB_PITCH, TT), :],
            y_hbm.at[b, pl.ds(t0, TT), pl.ds(j * LANES, LANES)], sem_out.at[0])
            for b in range(nb) for j in range(D_MODEL // LANES)]

    @pl.when(step == 0)
    def _first_fetch():
        for cp in in_copies(0, 0):
            cp.start()

    @pl.when(step + 1 < n_steps)
    def _prefetch():
        for cp in in_copies(step + 1, 1 - slot):
            cp.start()

    @pl.when(step == 0)
    def _init():
        vp_s[0:ha, :] = jnp.zeros((ha, W_A), F32)
        for c in range(NC_D):
            zb_s[CH_BX + c, 0:hb, :] = jnp.zeros((hb, MXU_N), F32)
        h_s[...] = jnp.zeros_like(h_s)
        for k in range(CONV_A_WIDTH):
            wb_s[k * nb:(k + 1) * nb, :] = jnp.broadcast_to(wdw_ref[k:k + 1, :], (nb, W_A))

    @pl.when(step > 0)
    def _carry():
        vp_s[0:ha, :] = vp_s[rows:rows + ha, :]
        for c in range(NC_D):
            zb_s[CH_BX + c, 0:hb, :] = zb_s[CH_BX + c, rows:rows + hb, :]

    def tiled_loop(n_iter, tile_body, host_body=None):
        per_iter = n_tiles // n_iter

        def body(it, c):
            if host_body is not None:
                host_body(it)
            for q in range(per_iter):
                tile_body(_aligned((it * per_iter + q) * ROW_TILE, ROW_TILE))
            return c
        if host_body is not None:
            for it in range(n_iter):
                body(it, 0)
        else:
            lax.fori_loop(0, n_iter, body, 0)

    for cp in in_copies(step, slot):
        cp.wait()

    def time_rows(slab, groups, r0):
        t = lax.shift_right_logical(r0, SUBLANES.bit_length() - 1)
        return jnp.concatenate(
            [jnp.concatenate([slab[slot, j, pl.ds(t + q, nb, stride=SLAB_PITCH), :] for j in range(groups)],
                             axis=-1) for q in range(ROW_TILE // nb)], axis=0)

    def norm_in(r0):
        rs = pl.ds(r0, ROW_TILE)
        x = time_rows(xs_s, D_MODEL // LANES, r0)
        xt_s[rs, :] = x
        u_s[rs, :] = _rmsnorm(x, gn_ref[...]).astype(BF16)
        p16_s[rs, :] = time_rows(ps_s, PLE_DIM // LANES, r0).astype(BF16)
    tiled_loop(2, norm_in)

    for c in range(NC_GLU):
        z_s[c] = _dot(u_s[...], win_ref[:, _chunk_cols(c)])

    def glu(r0):
        rs = pl.ds(r0, ROW_TILE)
        val = _lanes_of_chunks(z_s, 0, rs)
        gl = _lanes_of_chunks(z_s, NC_D, rs)
        vp_s[pl.ds(ha + r0, ROW_TILE), :] = val * _sigmoid(gl)
    tiled_loop(8, glu)

    def conv_piece(blk, j0, j1):
        base = blk * (CONV_T * nb)
        for j in range(j0, j1):
            ls = slice(j * LANES, (j + 1) * LANES)
            bias = jnp.broadcast_to(bdw_ref[:, ls], (nb, LANES))
            acc = [bias] * CONV_T
            taps = [wb_s[k * nb:(k + 1) * nb, ls] for k in range(CONV_A_WIDTH)]
            for d in range(CONV_T + CONV_A_WIDTH - 1):
                off = (d + HIST_A - (CONV_A_WIDTH - 1)) * nb
                xd = vp_s[pl.ds(base + off, nb), ls]
                for s in range(CONV_T):
                    k = d - s
                    if 0 <= k < CONV_A_WIDTH:
                        acc[s] = acc[s] + taps[k] * xd
            for s in range(CONV_T):
                cv_s[pl.ds(base + s * nb, nb), ls] = acc[s]

    def in_dot(ch):
        zb_s[ch, cur, :] = _dot(u_s[...], win_ref[:, _chunk_cols(NC_GLU + ch)])

    pieces = [(blk, j0, j0 + CONV_J) for blk in range(TT // CONV_T) for j0 in range(0, W_A // LANES, CONV_J)]
    next_dot = 0
    for n, piece in enumerate(pieces):
        want = -(-(n + 1) * NC_REST // len(pieces))
        while next_dot < want:
            in_dot(next_dot)
            next_dot += 1
        conv_piece(*piece)

    def conv_b(r0):
        rs = pl.ds(r0, ROW_TILE)
        for h in range(N_B_HEADS):
            ls = slice(h * HEAD_B, (h + 1) * HEAD_B)
            ch = CH_BX + (h * HEAD_B) // MXU_N
            lo = (h * HEAD_B) % MXU_N
            acc = jnp.broadcast_to(bcb_ref[:, ls], (ROW_TILE, HEAD_B))
            for k in range(CONV_B_WIDTH):
                off = (k + HIST_B - (CONV_B_WIDTH - 1)) * nb
                acc = acc + wcb_ref[k:k + 1, ls] * zb_s[ch, pl.ds(r0 + off, ROW_TILE), lo:lo + HEAD_B]
            a_s[rs, ls] = acc
            xb16_s[h, rs, :] = acc.astype(BF16)
    tiled_loop(8, conv_b)

    def gate_dot(it):
        for q in range(2):
            h = it * 2 + q
            z_s[h] = _dot(xb16_s[h], wri_ref[h])

    def mix_a(r0):
        rs = pl.ds(r0, ROW_TILE)
        v = _layernorm(cv_s[rs, :], lng_ref[...], lnb_ref[...])
        gate = _lanes_of_chunks(zb_s, CH_GATE_A, pl.ds(hb + r0, ROW_TILE))
        mixa_s[rs, :] = (_silu(v) * _silu(gate)).astype(BF16)
    tiled_loop(N_B_HEADS // 2, mix_a, gate_dot)

    c_lam = -RG_C * jax.nn.softplus(-lam_ref[...])

    def out_a_dot(c):
        zb_s[CH_GATE_A + c, cur, :] = _dot(mixa_s[...], wout_ref[0:W_A, _chunk_cols(c)])

    def coeffs(r0):
        rs = pl.ds(r0, ROW_TILE)
        for h in range(N_B_HEADS):
            ls = slice(h * HEAD_B, (h + 1) * HEAD_B)
            r_pre = z_s[h, rs, 0:HEAD_B] + br_ref[:, ls]
            i_pre = z_s[h, rs, HEAD_B:2 * HEAD_B] + bi_ref[:, ls]
            a, bx = _rglru_coeffs(r_pre, i_pre, a_s[rs, ls], c_lam[:, ls])
            a_s[rs, ls] = a
            b_s[rs, ls] = bx
    tiled_loop(NC_D, coeffs, out_a_dot)

    def scan(t, h):
        r0 = pl.multiple_of(t * nb, nb)
        h = a_s[pl.ds(r0, nb), :] * h + b_s[pl.ds(r0, nb), :]
        b_s[pl.ds(r0, nb), :] = h
        return h
    h_s[...] = lax.fori_loop(0, TT, scan, h_s[...], unroll=8)

    def mix_b(r0):
        rs = pl.ds(r0, ROW_TILE)
        gate = _lanes_of_chunks(zb_s, CH_GATE_B, pl.ds(hb + r0, ROW_TILE))
        mixb_s[rs, :] = (b_s[rs, :] * _silu(gate)).astype(BF16)
    tiled_loop(8, mix_b)

    for c in range(NC_D):
        z_s[c] = _dot(mixb_s[...], wout_ref[W_A:W_A + W_B, _chunk_cols(c)])

    def resid(r0):
        rs = pl.ds(r0, ROW_TILE)
        h1 = (xt_s[rs, :] + _lanes_of_chunks(zb_s, CH_GATE_A, pl.ds(hb + r0, ROW_TILE))
              + _lanes_of_chunks(z_s, 0, rs))
        cv_s[rs, :] = h1
        u_s[rs, :] = h1.astype(BF16)
    tiled_loop(8, resid)

    for c in range(NC_D):
        z_s[c] = _dot(u_s[...], wpg_ref[:, _chunk_cols(c)])
        z_s[NC_D + c] = _dot(p16_s[...], wpe_ref[:, _chunk_cols(c)])

    @pl.when(step > 0)
    def _drain_prev():
        for cp in out_copies(step - 1):
            cp.wait()

    def finish(r0):
        rs = pl.ds(r0, ROW_TILE)
        pg = _lanes_of_chunks(z_s, 0, rs)
        pe = _lanes_of_chunks(z_s, NC_D, rs)
        y = _rmsnorm(cv_s[rs, :] + pe * _sigmoid(pg), gf_ref[...])
        t = lax.shift_right_logical(r0, SUBLANES.bit_length() - 1)
        for q in range(ROW_TILE // nb):
            for j in range(D_MODEL // LANES):
                ys_s[j, pl.ds(t + q, nb, stride=SLAB_PITCH), :] = y[q * nb:(q + 1) * nb, j * LANES:(j + 1) * LANES]
    tiled_loop(2, finish)

    for cp in out_copies(step):
        cp.start()

    @pl.when(step == n_steps - 1)
    def _drain_last():
        for cp in out_copies(step):
            cp.wait()

    @pl.when(step == n_steps - 1)
    def _state_out():
        na = (CONV_A_WIDTH - 1) * nb
        nbb = (CONV_B_WIDTH - 1) * nb
        na_ref[...] = vp_s[ha + rows - na:ha + rows, :]
        for c in range(NC_D):
            nbuf_ref[:, c * MXU_N:(c + 1) * MXU_N] = zb_s[CH_BX + c, hb + rows - nbb:hb + rows, :]
        nh_ref[...] = h_s[...]


def _sample_kernel(x_ref, p_ref, sa_ref, sb_ref, h0_ref, gn_ref, win_ref, wdw_ref, bdw_ref,
                   lng_ref, lnb_ref, wcb_ref, bcb_ref, wri_ref, br_ref, bi_ref, lam_ref,
                   wout_ref, wpe_ref, wpg_ref, gf_ref,
                   y_ref, na_ref, nbuf_ref, nh_ref,
                   u_s, z_s, v_s, cvp_s):
    step = pl.program_id(0)
    ka = CONV_A_WIDTH - 1
    kb = CONV_B_WIDTH - 1

    @pl.when(step == 0)
    def _project():
        u = _rmsnorm(x_ref[...], gn_ref[...]).astype(BF16)
        u_s[...] = u
        za = _dot(u, win_ref[:, 0:2 * W_A])
        v_s[...] = za[:, 0:W_A] * _sigmoid(za[:, W_A:2 * W_A])
        z_s[...] = _dot(u, win_ref[:, 2 * W_A:W_IN])

    w_hist = wdw_ref[0:ka, :]

    def per_seq(b, c):
        row = step * SAMPLE_BT + b
        st = sa_ref[b]
        cvp_s[pl.ds(row, 1), :] = jnp.sum(st * w_hist, axis=0, keepdims=True)
        na_ref[b, 0:ka - 1, :] = sa_ref[b, 1:ka, :]
        na_ref[b, ka - 1:ka, :] = v_s[pl.ds(row, 1), :]
        return c
    lax.fori_loop(0, SAMPLE_BT, per_seq, 0)

    @pl.when(step == pl.num_programs(0) - 1)
    def _rest():
        v = v_s[...]
        cv = cvp_s[...] + wdw_ref[ka:ka + 1, :] * v + bdw_ref[...]
        v = _silu(_layernorm(cv, lng_ref[...], lnb_ref[...])) * _silu(z_s[:, 0:W_A])
        mix_a = v.astype(BF16)

        b_x = z_s[:, W_A:W_A + W_B]
        xb = wcb_ref[kb:kb + 1, :] * b_x + bcb_ref[...]
        for k in range(kb):
            xb = xb + wcb_ref[k:k + 1, :] * sb_ref[k]
        for k in range(kb - 1):
            nbuf_ref[k] = sb_ref[k + 1]
        nbuf_ref[kb - 1] = b_x

        xb16 = xb.astype(BF16)
        c_lam = -RG_C * jax.nn.softplus(-lam_ref[...])
        a_parts, b_parts = [], []
        for h in range(N_B_HEADS):
            ls = slice(h * HEAD_B, (h + 1) * HEAD_B)
            g = _dot(xb16[:, ls], wri_ref[h])
            a, bx = _rglru_coeffs(g[:, 0:HEAD_B] + br_ref[:, ls], g[:, HEAD_B:2 * HEAD_B] + bi_ref[:, ls],
                                  xb[:, ls], c_lam[:, ls])
            a_parts.append(a)
            b_parts.append(bx)
        hn = jnp.concatenate(a_parts, axis=-1) * h0_ref[...] + jnp.concatenate(b_parts, axis=-1)
        nh_ref[...] = hn
        mix_b = (hn * _silu(z_s[:, W_A + W_B:W_A + 2 * W_B])).astype(BF16)

        mix = _dot(mix_a, wout_ref[0:W_A, :]) + _dot(mix_b, wout_ref[W_A:W_A + W_B, :])
        h1 = x_ref[...] + mix
        pe = _dot(p_ref[...].astype(BF16), wpe_ref[...]) * _sigmoid(_dot(h1.astype(BF16), wpg_ref[...]))
        y_ref[...] = _rmsnorm(h1 + pe, gf_ref[...])


def _cast_kernel(*refs):
    n = len(refs) // 2
    for src, dst in zip(refs[:n], refs[n:]):
        dst[...] = src[...].astype(BF16)


def _cast_weights(*ws):
    spec = lambda w: pl.BlockSpec((w.shape[0] // CAST_STEPS, w.shape[1]), lambda i: (i, 0))
    return pl.pallas_call(
        _cast_kernel,
        grid=(CAST_STEPS,),
        in_specs=[spec(w) for w in ws],
        out_specs=[spec(w) for w in ws],
        out_shape=[jax.ShapeDtypeStruct(w.shape, BF16) for w in ws],
        compiler_params=pltpu.CompilerParams(dimension_semantics=("arbitrary",)),
        name="cast_weights",
    )(*ws)


def _const_spec(shape):
    zeros = (0,) * len(shape)
    return pl.BlockSpec(shape, lambda i: zeros, pipeline_mode=pl.Buffered(1))


def _weight_specs():
    return [
        _const_spec((1, D_MODEL)),
        _const_spec((D_MODEL, W_IN)),
        _const_spec((CONV_A_WIDTH, W_A)),
        _const_spec((1, W_A)),
        _const_spec((1, W_A)),
        _const_spec((1, W_A)),
        _const_spec((CONV_B_WIDTH, W_B)),
        _const_spec((1, W_B)),
        _const_spec((N_B_HEADS, HEAD_B, 2 * HEAD_B)),
        _const_spec((1, W_B)),
        _const_spec((1, W_B)),
        _const_spec((1, W_B)),
        _const_spec((W_A + W_B, D_MODEL)),
        _const_spec((PLE_DIM, D_MODEL)),
        _const_spec((D_MODEL, D_MODEL)),
        _const_spec((1, D_MODEL)),
    ]


def _prompt_call(x, p, weights):
    nb, seq, _ = x.shape
    rows = TT * nb
    n_a = (CONV_A_WIDTH - 1) * nb
    n_b = (CONV_B_WIDTH - 1) * nb
    slab_rows = nb * SLAB_PITCH
    hbm = pl.BlockSpec(memory_space=pl.ANY)
    out_shape = (
        jax.ShapeDtypeStruct((nb, seq, D_MODEL), F32),
        jax.ShapeDtypeStruct((n_a, W_A), F32),
        jax.ShapeDtypeStruct((n_b, W_B), F32),
        jax.ShapeDtypeStruct((nb, W_B), F32),
    )
    const_out = lambda shape: pl.BlockSpec(shape, lambda i: (0, 0))
    return pl.pallas_call(
        functools.partial(_prompt_kernel, nb),
        grid=(seq // TT,),
        in_specs=[hbm, hbm] + _weight_specs(),
        out_specs=(hbm, const_out((n_a, W_A)), const_out((n_b, W_B)), const_out((nb, W_B))),
        out_shape=out_shape,
        scratch_shapes=[
            pltpu.VMEM((rows, D_MODEL), BF16),
            pltpu.VMEM((NC_GLU, rows, MXU_N), F32),
            pltpu.VMEM((NC_REST, HIST_B * nb + rows, MXU_N), F32),
            pltpu.VMEM(((HIST_A + TT) * nb, W_A), F32),
            pltpu.VMEM((rows, D_MODEL), F32),
            pltpu.VMEM((rows, W_B), F32),
            pltpu.VMEM((rows, W_B), F32),
            pltpu.VMEM((rows, W_A), BF16),
            pltpu.VMEM((rows, W_B), BF16),
            pltpu.VMEM((N_B_HEADS, rows, HEAD_B), BF16),
            pltpu.VMEM((nb, W_B), F32),
            pltpu.VMEM((CONV_A_WIDTH * nb, W_A), F32),
            pltpu.VMEM((2, D_MODEL // LANES, slab_rows, LANES), F32),
            pltpu.VMEM((2, PLE_DIM // LANES, slab_rows, LANES), F32),
            pltpu.VMEM((D_MODEL // LANES, slab_rows, LANES), F32),
            pltpu.VMEM((rows, D_MODEL), F32),
            pltpu.VMEM((rows, PLE_DIM), BF16),
            pltpu.SemaphoreType.DMA((2,)),
            pltpu.SemaphoreType.DMA((1,)),
        ],
        compiler_params=pltpu.CompilerParams(
            dimension_semantics=("arbitrary",), vmem_limit_bytes=VMEM_LIMIT_BYTES),
        name="prompt_layer",
    )(x, p, *weights)


def _sample_call(x_s, p_s, sa, sb_t, h0, weights):
    n = x_s.shape[0]
    ka = CONV_A_WIDTH - 1
    kb = CONV_B_WIDTH - 1
    out_shape = (
        jax.ShapeDtypeStruct((n, D_MODEL), F32),
        jax.ShapeDtypeStruct((n, ka, W_A), F32),
        jax.ShapeDtypeStruct((kb, n, W_B), F32),
        jax.ShapeDtypeStruct((n, W_B), F32),
    )
    full2 = lambda shape: pl.BlockSpec(shape, lambda i: (0, 0))
    full3 = lambda shape: pl.BlockSpec(shape, lambda i: (0, 0, 0))
    return pl.pallas_call(
        _sample_kernel,
        grid=(n // SAMPLE_BT,),
        in_specs=[full2((n, D_MODEL)), full2((n, PLE_DIM)),
                  pl.BlockSpec((SAMPLE_BT, ka, W_A), lambda i: (i, 0, 0)),
                  full3((kb, n, W_B)), full2((n, W_B))] + _weight_specs(),
        out_specs=(full2((n, D_MODEL)),
                   pl.BlockSpec((SAMPLE_BT, ka, W_A), lambda i: (i, 0, 0)),
                   full3((kb, n, W_B)), full2((n, W_B))),
        out_shape=out_shape,
        scratch_shapes=[
            pltpu.VMEM((n, D_MODEL), BF16),
            pltpu.VMEM((n, W_IN - 2 * W_A), F32),
            pltpu.VMEM((n, W_A), F32),
            pltpu.VMEM((n, W_A), F32),
        ],
        compiler_params=pltpu.CompilerParams(
            dimension_semantics=("arbitrary",), vmem_limit_bytes=VMEM_LIMIT_BYTES),
        name="sample_layer",
    )(x_s, p_s, sa, sb_t, h0, *weights)


def kernel(x_prompt, x_sample, p_prompt, p_sample, state_conv_a, state_conv_b, state_h, g_norm, w_in, w_dw_a, b_dw_a, ln_g, ln_b, w_conv_b, b_conv_b, w_r, b_r, w_i, b_i, lam, w_out, w_pe, w_pg, g_final):
    depth = g_norm.shape[0]
    assert depth == 1, "single-layer step"
    nb, seq, _ = x_prompt.shape
    assert nb == SUBLANES and seq % TT == 0 and TT >= HIST_A
    row = lambda a: a.reshape(1, -1)
    win16, wout16, wpe16, wpg16 = _cast_weights(w_in[0], w_out[0], w_pe[0], w_pg[0])
    weights = (
        row(g_norm[0]), win16,
        w_dw_a[0], row(b_dw_a[0]), row(ln_g[0]), row(ln_b[0]),
        w_conv_b[0], row(b_conv_b[0]),
        jnp.concatenate([w_r[0], w_i[0]], axis=-1).astype(BF16), row(b_r[0]), row(b_i[0]), row(lam[0]),
        wout16, wpe16, wpg16, row(g_final),
    )

    y_prompt, na_t, nbuf_t, nh_p = _prompt_call(x_prompt, p_prompt[0], weights)
    untime = lambda a, k: jnp.swapaxes(a.reshape(k, nb, a.shape[-1]), 0, 1)
    na_p = untime(na_t, CONV_A_WIDTH - 1)[None]
    nb_p = untime(nbuf_t, CONV_B_WIDTH - 1)[None]

    n_s = x_sample.shape[0]
    assert x_sample.shape[1] == 1 and n_s % SAMPLE_BT == 0
    y_s, na_s, nbuf_s, nh_s = _sample_call(
        x_sample.reshape(n_s, D_MODEL), p_sample[0].reshape(n_s, PLE_DIM), state_conv_a[0],
        jnp.swapaxes(state_conv_b[0], 0, 1), state_h[0], weights)
    return (y_prompt, y_s.reshape(n_s, 1, D_MODEL), na_p, nb_p, nh_p[None],
            na_s[None], jnp.swapaxes(nbuf_s, 0, 1)[None], nh_s[None])
```

```python
import functools

import jax
import jax.numpy as jnp
from jax import lax
from jax.experimental import pallas as pl
from jax.experimental.pallas import tpu as pltpu

D_MODEL = 1024
W_A = D_MODEL
W_B = D_MODEL
N_B_HEADS = 8
HEAD_B = W_B // N_B_HEADS
W_IN = 3 * W_A + 2 * W_B
CONV_A_WIDTH = 31
CONV_B_WIDTH = 4
RG_C = 8.0
PLE_DIM = 256
EPS = 1e-6

SUBLANES = 8
LANES = 128
MXU_N = 256
VMEM_LIMIT_BYTES = 60000 * 1024

TT = 64
HIST_A = 32
HIST_B = 8
SLAB_PITCH = TT + 8
ROW_TILE = 16
CONV_T = 8
CONV_J = 2
SAMPLE_BT = 16
CAST_STEPS = 8

NC_D = D_MODEL // MXU_N
NC_GLU = 2 * W_A // MXU_N
NC_REST = (W_IN - 2 * W_A) // MXU_N
CH_GATE_A = 0
CH_BX = W_A // MXU_N
CH_GATE_B = (W_A + W_B) // MXU_N

F32 = jnp.float32
BF16 = jnp.bfloat16


def _sigmoid(x):
    return 0.5 * jnp.tanh(0.5 * x) + 0.5


def _silu(x):
    hx = 0.5 * x
    return hx * jnp.tanh(hx) + hx


def _dot(a, b):
    return jnp.dot(a, b, preferred_element_type=F32)


def _rmsnorm(x, g):
    ms = jnp.mean(x * x, axis=-1, keepdims=True)
    return x * lax.rsqrt(ms + EPS) * g


def _layernorm(x, g, b):
    mu = jnp.mean(x, axis=-1, keepdims=True)
    xc = x - mu
    var = jnp.mean(xc * xc, axis=-1, keepdims=True)
    return xc * lax.rsqrt(var + EPS) * g + b


def _rglru_coeffs(r_pre, i_pre, xb, c_lam):
    r = _sigmoid(r_pre)
    ig = _sigmoid(i_pre)
    log_a = c_lam * r
    a = jnp.exp(log_a)
    th = jnp.tanh(log_a)
    bx = jnp.sqrt(-2.0 * th / (1.0 - th)) * (ig * xb)
    return a, bx


def _aligned(v, m):
    return v if isinstance(v, int) else pl.multiple_of(v, m)


def _chunk_cols(c):
    return slice(c * MXU_N, (c + 1) * MXU_N)


def _lanes_of_chunks(ref, first, rows_sl):
    return jnp.concatenate([ref[first + c, rows_sl, :] for c in range(NC_D)], axis=-1)


def _prompt_kernel(nb, x_hbm, p_hbm, gn_ref, win_ref, wdw_ref, bdw_ref, lng_ref, lnb_ref,
                   wcb_ref, bcb_ref, wri_ref, br_ref, bi_ref, lam_ref, wout_ref, wpe_ref,
                   wpg_ref, gf_ref,
                   y_hbm, na_ref, nbuf_ref, nh_ref,
                   u_s, z_s, zb_s, vp_s, cv_s, a_s, b_s, mixa_s, mixb_s, xb16_s, h_s, wb_s,
                   xs_s, ps_s, ys_s, xt_s, p16_s, sem_in, sem_out):
    rows = TT * nb
    ha = HIST_A * nb
    hb = HIST_B * nb
    step = pl.program_id(0)
    n_steps = pl.num_programs(0)
    n_tiles = rows // ROW_TILE
    cur = slice(hb, hb + rows)
    slot = step % 2

    def in_copies(s, sl):
        t0 = pl.multiple_of(s * TT, TT)
        cps = []
        for b in range(nb):
            dst_rows = pl.ds(b * SLAB_PITCH, TT)
            for j in range(D_MODEL // LANES):
                cps.append(pltpu.make_async_copy(
                    x_hbm.at[b, pl.ds(t0, TT), pl.ds(j * LANES, LANES)],
                    xs_s.at[sl, j, dst_rows, :], sem_in.at[sl]))
            for j in range(PLE_DIM // LANES):
                cps.append(pltpu.make_async_copy(
                    p_hbm.at[b, pl.ds(t0, TT), pl.ds(j * LANES, LANES)],
                    ps_s.at[sl, j, dst_rows, :], sem_in.at[sl]))
        return cps

    def out_copies(s):
        t0 = pl.multiple_of(s * TT, TT)
        return [pltpu.make_async_copy(
            ys_s.at[j, pl.ds(b * SLAB_PITCH, TT), :],
            y_hbm.at[b, pl.ds(t0, TT), pl.ds(j * LANES, LANES)], sem_out.at[0])
            for b in range(nb) for j in range(D_MODEL // LANES)]

    @pl.when(step == 0)
    def _first_fetch():
        for cp in in_copies(0, 0):
            cp.start()

    @pl.when(step + 1 < n_steps)
    def _prefetch():
        for cp in in_copies(step + 1, 1 - slot):
            cp.start()

    @pl.when(step == 0)
    def _init():
        vp_s[0:ha, :] = jnp.zeros((ha, W_A), F32)
        for c in range(NC_D):
            zb_s[CH_BX + c, 0:hb, :] = jnp.zeros((hb, MXU_N), F32)
        h_s[...] = jnp.zeros_like(h_s)
        for k in range(CONV_A_WIDTH):
            wb_s[k * nb:(k + 1) * nb, :] = jnp.broadcast_to(wdw_ref[k:k + 1, :], (nb, W_A))

    @pl.when(step > 0)
    def _carry():
        vp_s[0:ha, :] = vp_s[rows:rows + ha, :]
        for c in range(NC_D):
            zb_s[CH_BX + c, 0:hb, :] = zb_s[CH_BX + c, rows:rows + hb, :]

    def tiled_loop(n_iter, tile_body, host_body=None):
        per_iter = n_tiles // n_iter

        def body(it, c):
            if host_body is not None:
                host_body(it)
            for q in range(per_iter):
                tile_body(_aligned((it * per_iter + q) * ROW_TILE, ROW_TILE))
            return c
        if host_body is not None:
            for it in range(n_iter):
                body(it, 0)
        else:
            lax.fori_loop(0, n_iter, body, 0)

    for cp in in_copies(step, slot):
        cp.wait()

    def time_rows(slab, groups, r0):
        t = lax.shift_right_logical(r0, SUBLANES.bit_length() - 1)
        return jnp.concatenate(
            [jnp.concatenate([slab[slot, j, pl.ds(t + q, nb, stride=SLAB_PITCH), :] for j in range(groups)],
                             axis=-1) for q in range(ROW_TILE // nb)], axis=0)

    def norm_in(r0):
        rs = pl.ds(r0, ROW_TILE)
        x = time_rows(xs_s, D_MODEL // LANES, r0)
        xt_s[rs, :] = x
        u_s[rs, :] = _rmsnorm(x, gn_ref[...]).astype(BF16)
        p16_s[rs, :] = time_rows(ps_s, PLE_DIM // LANES, r0).astype(BF16)
    tiled_loop(2, norm_in)

    for c in range(NC_GLU):
        z_s[c] = _dot(u_s[...], win_ref[:, _chunk_cols(c)])

    def glu(r0):
        rs = pl.ds(r0, ROW_TILE)
        val = _lanes_of_chunks(z_s, 0, rs)
        gl = _lanes_of_chunks(z_s, NC_D, rs)
        vp_s[pl.ds(ha + r0, ROW_TILE), :] = val * _sigmoid(gl)
    tiled_loop(8, glu)

    def conv_piece(blk, j0, j1):
        base = blk * (CONV_T * nb)
        for j in range(j0, j1):
            ls = slice(j * LANES, (j + 1) * LANES)
            bias = jnp.broadcast_to(bdw_ref[:, ls], (nb, LANES))
            acc = [bias] * CONV_T
            taps = [wb_s[k * nb:(k + 1) * nb, ls] for k in range(CONV_A_WIDTH)]
            for d in range(CONV_T + CONV_A_WIDTH - 1):
                off = (d + HIST_A - (CONV_A_WIDTH - 1)) * nb
                xd = vp_s[pl.ds(base + off, nb), ls]
                for s in range(CONV_T):
                    k = d - s
                    if 0 <= k < CONV_A_WIDTH:
                        acc[s] = acc[s] + taps[k] * xd
            for s in range(CONV_T):
                cv_s[pl.ds(base + s * nb, nb), ls] = acc[s]

    def in_dot(ch):
        zb_s[ch, cur, :] = _dot(u_s[...], win_ref[:, _chunk_cols(NC_GLU + ch)])

    pieces = [(blk, j0, j0 + CONV_J) for blk in range(TT // CONV_T) for j0 in range(0, W_A // LANES, CONV_J)]
    next_dot = 0
    for n, piece in enumerate(pieces):
        want = -(-(n + 1) * NC_REST // len(pieces))
        while next_dot < want:
            in_dot(next_dot)
            next_dot += 1
        conv_piece(*piece)

    def conv_b(r0):
        rs = pl.ds(r0, ROW_TILE)
        for h in range(N_B_HEADS):
            ls = slice(h * HEAD_B, (h + 1) * HEAD_B)
            ch = CH_BX + (h * HEAD_B) // MXU_N
            lo = (h * HEAD_B) % MXU_N
            acc = jnp.broadcast_to(bcb_ref[:, ls], (ROW_TILE, HEAD_B))
            for k in range(CONV_B_WIDTH):
                off = (k + HIST_B - (CONV_B_WIDTH - 1)) * nb
                acc = acc + wcb_ref[k:k + 1, ls] * zb_s[ch, pl.ds(r0 + off, ROW_TILE), lo:lo + HEAD_B]
            a_s[rs, ls] = acc
            xb16_s[h, rs, :] = acc.astype(BF16)
    tiled_loop(8, conv_b)

    def gate_dot(it):
        for q in range(2):
            h = it * 2 + q
            z_s[h] = _dot(xb16_s[h], wri_ref[h])

    def mix_a(r0):
        rs = pl.ds(r0, ROW_TILE)
        v = _layernorm(cv_s[rs, :], lng_ref[...], lnb_ref[...])
        gate = _lanes_of_chunks(zb_s, CH_GATE_A, pl.ds(hb + r0, ROW_TILE))
        mixa_s[rs, :] = (_silu(v) * _silu(gate)).astype(BF16)
    tiled_loop(N_B_HEADS // 2, mix_a, gate_dot)

    c_lam = -RG_C * jax.nn.softplus(-lam_ref[...])

    def out_a_dot(c):
        zb_s[CH_GATE_A + c, cur, :] = _dot(mixa_s[...], wout_ref[0:W_A, _chunk_cols(c)])

    heads = [slice(h * HEAD_B, (h + 1) * HEAD_B) for h in range(N_B_HEADS)]
    h_run = [h_s[:, ls] for ls in heads]

    def coeffs(r0):
        rs = pl.ds(r0, ROW_TILE)
        for h, ls in enumerate(heads):
            r_pre = z_s[h, rs, 0:HEAD_B] + br_ref[:, ls]
            i_pre = z_s[h, rs, HEAD_B:2 * HEAD_B] + bi_ref[:, ls]
            a, bx = _rglru_coeffs(r_pre, i_pre, a_s[rs, ls], c_lam[:, ls])
            for q in range(ROW_TILE // nb):
                tq = slice(q * nb, (q + 1) * nb)
                h_run[h] = a[tq] * h_run[h] + bx[tq]
                b_s[pl.ds(r0 + q * nb, nb), ls] = h_run[h]
    tiled_loop(NC_D, coeffs, out_a_dot)
    for h, ls in enumerate(heads):
        h_s[:, ls] = h_run[h]

    def mix_b(r0):
        rs = pl.ds(r0, ROW_TILE)
        gate = _lanes_of_chunks(zb_s, CH_GATE_B, pl.ds(hb + r0, ROW_TILE))
        mixb_s[rs, :] = (b_s[rs, :] * _silu(gate)).astype(BF16)
    tiled_loop(8, mix_b)

    for c in range(NC_D):
        z_s[c] = _dot(mixb_s[...], wout_ref[W_A:W_A + W_B, _chunk_cols(c)])

    def resid(r0):
        rs = pl.ds(r0, ROW_TILE)
        h1 = (xt_s[rs, :] + _lanes_of_chunks(zb_s, CH_GATE_A, pl.ds(hb + r0, ROW_TILE))
              + _lanes_of_chunks(z_s, 0, rs))
        cv_s[rs, :] = h1
        u_s[rs, :] = h1.astype(BF16)
    tiled_loop(8, resid)

    for c in range(NC_D):
        z_s[c] = _dot(u_s[...], wpg_ref[:, _chunk_cols(c)])
        z_s[NC_D + c] = _dot(p16_s[...], wpe_ref[:, _chunk_cols(c)])

    @pl.when(step > 0)
    def _drain_prev():
        for cp in out_copies(step - 1):
            cp.wait()

    def finish(r0):
        rs = pl.ds(r0, ROW_TILE)
        pg = _lanes_of_chunks(z_s, 0, rs)
        pe = _lanes_of_chunks(z_s, NC_D, rs)
        y = _rmsnorm(cv_s[rs, :] + pe * _sigmoid(pg), gf_ref[...])
        t = lax.shift_right_logical(r0, SUBLANES.bit_length() - 1)
        for q in range(ROW_TILE // nb):
            for j in range(D_MODEL // LANES):
                ys_s[j, pl.ds(t + q, nb, stride=SLAB_PITCH), :] = y[q * nb:(q + 1) * nb, j * LANES:(j + 1) * LANES]
    tiled_loop(2, finish)

    for cp in out_copies(step):
        cp.start()

    @pl.when(step == n_steps - 1)
    def _drain_last():
        for cp in out_copies(step):
            cp.wait()

    @pl.when(step == n_steps - 1)
    def _state_out():
        na = (CONV_A_WIDTH - 1) * nb
        nbb = (CONV_B_WIDTH - 1) * nb
        na_ref[...] = vp_s[ha + rows - na:ha + rows, :]
        for c in range(NC_D):
            nbuf_ref[:, c * MXU_N:(c + 1) * MXU_N] = zb_s[CH_BX + c, hb + rows - nbb:hb + rows, :]
        nh_ref[...] = h_s[...]


def _sample_kernel(x_ref, p_ref, sa_ref, sb_ref, h0_ref, gn_ref, win_ref, wdw_ref, bdw_ref,
                   lng_ref, lnb_ref, wcb_ref, bcb_ref, wri_ref, br_ref, bi_ref, lam_ref,
                   wout_ref, wpe_ref, wpg_ref, gf_ref,
                   y_ref, na_ref, nbuf_ref, nh_ref,
                   u_s, z_s, v_s, cvp_s):
    step = pl.program_id(0)
    ka = CONV_A_WIDTH - 1
    kb = CONV_B_WIDTH - 1

    @pl.when(step == 0)
    def _project():
        u = _rmsnorm(x_ref[...], gn_ref[...]).astype(BF16)
        u_s[...] = u
        za = _dot(u, win_ref[:, 0:2 * W_A])
        v_s[...] = za[:, 0:W_A] * _sigmoid(za[:, W_A:2 * W_A])
        z_s[...] = _dot(u, win_ref[:, 2 * W_A:W_IN])

    seqs = pl.ds(pl.multiple_of(step * SAMPLE_BT, SAMPLE_BT), SAMPLE_BT)
    acc = wdw_ref[0:1, :] * sa_ref[0]
    for k in range(1, ka):
        acc = acc + wdw_ref[k:k + 1, :] * sa_ref[k]
        na_ref[k - 1] = sa_ref[k]
    cvp_s[seqs, :] = acc
    na_ref[ka - 1] = v_s[seqs, :]

    @pl.when(step == pl.num_programs(0) - 1)
    def _rest():
        v = v_s[...]
        cv = cvp_s[...] + wdw_ref[ka:ka + 1, :] * v + bdw_ref[...]
        v = _silu(_layernorm(cv, lng_ref[...], lnb_ref[...])) * _silu(z_s[:, 0:W_A])
        mix_a = v.astype(BF16)

        b_x = z_s[:, W_A:W_A + W_B]
        xb = wcb_ref[kb:kb + 1, :] * b_x + bcb_ref[...]
        for k in range(kb):
            xb = xb + wcb_ref[k:k + 1, :] * sb_ref[k]
        for k in range(kb - 1):
            nbuf_ref[k] = sb_ref[k + 1]
        nbuf_ref[kb - 1] = b_x

        xb16 = xb.astype(BF16)
        c_lam = -RG_C * jax.nn.softplus(-lam_ref[...])
        a_parts, b_parts = [], []
        for h in range(N_B_HEADS):
            ls = slice(h * HEAD_B, (h + 1) * HEAD_B)
            g = _dot(xb16[:, ls], wri_ref[h])
            a, bx = _rglru_coeffs(g[:, 0:HEAD_B] + br_ref[:, ls], g[:, HEAD_B:2 * HEAD_B] + bi_ref[:, ls],
                                  xb[:, ls], c_lam[:, ls])
            a_parts.append(a)
            b_parts.append(bx)
        hn = jnp.concatenate(a_parts, axis=-1) * h0_ref[...] + jnp.concatenate(b_parts, axis=-1)
        nh_ref[...] = hn
        mix_b = (hn * _silu(z_s[:, W_A + W_B:W_A + 2 * W_B])).astype(BF16)

        mix = _dot(mix_a, wout_ref[0:W_A, :]) + _dot(mix_b, wout_ref[W_A:W_A + W_B, :])
        h1 = x_ref[...] + mix
        pe = _dot(p_ref[...].astype(BF16), wpe_ref[...]) * _sigmoid(_dot(h1.astype(BF16), wpg_ref[...]))
        y_ref[...] = _rmsnorm(h1 + pe, gf_ref[...])


def _cast_kernel(*refs):
    n = len(refs) // 2
    for src, dst in zip(refs[:n], refs[n:]):
        dst[...] = src[...].astype(BF16)


def _cast_weights(*ws):
    spec = lambda w: pl.BlockSpec((w.shape[0] // CAST_STEPS, w.shape[1]), lambda i: (i, 0))
    return pl.pallas_call(
        _cast_kernel,
        grid=(CAST_STEPS,),
        in_specs=[spec(w) for w in ws],
        out_specs=[spec(w) for w in ws],
        out_shape=[jax.ShapeDtypeStruct(w.shape, BF16) for w in ws],
        compiler_params=pltpu.CompilerParams(dimension_semantics=("arbitrary",)),
        name="cast_weights",
    )(*ws)


def _const_spec(shape):
    zeros = (0,) * len(shape)
    return pl.BlockSpec(shape, lambda i: zeros, pipeline_mode=pl.Buffered(1))


def _weight_specs():
    return [
        _const_spec((1, D_MODEL)),
        _const_spec((D_MODEL, W_IN)),
        _const_spec((CONV_A_WIDTH, W_A)),
        _const_spec((1, W_A)),
        _const_spec((1, W_A)),
        _const_spec((1, W_A)),
        _const_spec((CONV_B_WIDTH, W_B)),
        _const_spec((1, W_B)),
        _const_spec((N_B_HEADS, HEAD_B, 2 * HEAD_B)),
        _const_spec((1, W_B)),
        _const_spec((1, W_B)),
        _const_spec((1, W_B)),
        _const_spec((W_A + W_B, D_MODEL)),
        _const_spec((PLE_DIM, D_MODEL)),
        _const_spec((D_MODEL, D_MODEL)),
        _const_spec((1, D_MODEL)),
    ]


def _prompt_call(x, p, weights):
    nb, seq, _ = x.shape
    rows = TT * nb
    n_a = (CONV_A_WIDTH - 1) * nb
    n_b = (CONV_B_WIDTH - 1) * nb
    slab_rows = nb * SLAB_PITCH
    hbm = pl.BlockSpec(memory_space=pl.ANY)
    out_shape = (
        jax.ShapeDtypeStruct((nb, seq, D_MODEL), F32),
        jax.ShapeDtypeStruct((n_a, W_A), F32),
        jax.ShapeDtypeStruct((n_b, W_B), F32),
        jax.ShapeDtypeStruct((nb, W_B), F32),
    )
    const_out = lambda shape: pl.BlockSpec(shape, lambda i: (0, 0))
    return pl.pallas_call(
        functools.partial(_prompt_kernel, nb),
        grid=(seq // TT,),
        in_specs=[hbm, hbm] + _weight_specs(),
        out_specs=(hbm, const_out((n_a, W_A)), const_out((n_b, W_B)), const_out((nb, W_B))),
        out_shape=out_shape,
        scratch_shapes=[
            pltpu.VMEM((rows, D_MODEL), BF16),
            pltpu.VMEM((NC_GLU, rows, MXU_N), F32),
            pltpu.VMEM((NC_REST, HIST_B * nb + rows, MXU_N), F32),
            pltpu.VMEM(((HIST_A + TT) * nb, W_A), F32),
            pltpu.VMEM((rows, D_MODEL), F32),
            pltpu.VMEM((rows, W_B), F32),
            pltpu.VMEM((rows, W_B), F32),
            pltpu.VMEM((rows, W_A), BF16),
            pltpu.VMEM((rows, W_B), BF16),
            pltpu.VMEM((N_B_HEADS, rows, HEAD_B), BF16),
            pltpu.VMEM((nb, W_B), F32),
            pltpu.VMEM((CONV_A_WIDTH * nb, W_A), F32),
            pltpu.VMEM((2, D_MODEL // LANES, slab_rows, LANES), F32),
            pltpu.VMEM((2, PLE_DIM // LANES, slab_rows, LANES), F32),
            pltpu.VMEM((D_MODEL // LANES, slab_rows, LANES), F32),
            pltpu.VMEM((rows, D_MODEL), F32),
            pltpu.VMEM((rows, PLE_DIM), BF16),
            pltpu.SemaphoreType.DMA((2,)),
            pltpu.SemaphoreType.DMA((1,)),
        ],
        compiler_params=pltpu.CompilerParams(
            dimension_semantics=("arbitrary",), vmem_limit_bytes=VMEM_LIMIT_BYTES),
        name="prompt_layer",
    )(x, p, *weights)


def _sample_call(x_s, p_s, sa, sb_t, h0, weights):
    n = x_s.shape[0]
    ka = CONV_A_WIDTH - 1
    kb = CONV_B_WIDTH - 1
    out_shape = (
        jax.ShapeDtypeStruct((n, D_MODEL), F32),
        jax.ShapeDtypeStruct((ka, n, W_A), F32),
        jax.ShapeDtypeStruct((kb, n, W_B), F32),
        jax.ShapeDtypeStruct((n, W_B), F32),
    )
    full2 = lambda shape: pl.BlockSpec(shape, lambda i: (0, 0))
    full3 = lambda shape: pl.BlockSpec(shape, lambda i: (0, 0, 0))
    return pl.pallas_call(
        _sample_kernel,
        grid=(n // SAMPLE_BT,),
        in_specs=[full2((n, D_MODEL)), full2((n, PLE_DIM)),
                  pl.BlockSpec((ka, SAMPLE_BT, W_A), lambda i: (0, i, 0)),
                  full3((kb, n, W_B)), full2((n, W_B))] + _weight_specs(),
        out_specs=(full2((n, D_MODEL)),
                   pl.BlockSpec((ka, SAMPLE_BT, W_A), lambda i: (0, i, 0)),
                   full3((kb, n, W_B)), full2((n, W_B))),
        out_shape=out_shape,
        scratch_shapes=[
            pltpu.VMEM((n, D_MODEL), BF16),
            pltpu.VMEM((n, W_IN - 2 * W_A), F32),
            pltpu.VMEM((n, W_A), F32),
            pltpu.VMEM((n, W_A), F32),
        ],
        compiler_params=pltpu.CompilerParams(
            dimension_semantics=("arbitrary",), vmem_limit_bytes=VMEM_LIMIT_BYTES),
        name="sample_layer",
    )(x_s, p_s, sa, sb_t, h0, *weights)


def kernel(x_prompt, x_sample, p_prompt, p_sample, state_conv_a, state_conv_b, state_h, g_norm, w_in, w_dw_a, b_dw_a, ln_g, ln_b, w_conv_b, b_conv_b, w_r, b_r, w_i, b_i, lam, w_out, w_pe, w_pg, g_final):
    depth = g_norm.shape[0]
    assert depth == 1, "single-layer step"
    nb, seq, _ = x_prompt.shape
    assert nb == SUBLANES and seq % TT == 0 and TT >= HIST_A
    row = lambda a: a.reshape(1, -1)
    win16, wout16, wpe16, wpg16 = _cast_weights(w_in[0], w_out[0], w_pe[0], w_pg[0])
    weights = (
        row(g_norm[0]), win16,
        w_dw_a[0], row(b_dw_a[0]), row(ln_g[0]), row(ln_b[0]),
        w_conv_b[0], row(b_conv_b[0]),
        jnp.concatenate([w_r[0], w_i[0]], axis=-1).astype(BF16), row(b_r[0]), row(b_i[0]), row(lam[0]),
        wout16, wpe16, wpg16, row(g_final),
    )

    y_prompt, na_t, nbuf_t, nh_p = _prompt_call(x_prompt, p_prompt[0], weights)
    untime = lambda a, k: jnp.swapaxes(a.reshape(k, nb, a.shape[-1]), 0, 1)
    na_p = untime(na_t, CONV_A_WIDTH - 1)[None]
    nb_p = untime(nbuf_t, CONV_B_WIDTH - 1)[None]

    n_s = x_sample.shape[0]
    assert x_sample.shape[1] == 1 and n_s % SAMPLE_BT == 0
    y_s, na_s, nbuf_s, nh_s = _sample_call(
        x_sample.reshape(n_s, D_MODEL), p_sample[0].reshape(n_s, PLE_DIM), jnp.swapaxes(state_conv_a[0], 0, 1),
        jnp.swapaxes(state_conv_b[0], 0, 1), state_h[0], weights)
    return (y_prompt, y_s.reshape(n_s, 1, D_MODEL), na_p, nb_p, nh_p[None],
            jnp.swapaxes(na_s, 0, 1)[None], jnp.swapaxes(nbuf_s, 0, 1)[None], nh_s[None])
```

```python
import functools

import jax
import jax.numpy as jnp
from jax import lax
from jax.experimental import pallas as pl
from jax.experimental.pallas import tpu as pltpu

D_MODEL = 1024
W_A = D_MODEL
W_B = D_MODEL
N_B_HEADS = 8
HEAD_B = W_B // N_B_HEADS
W_IN = 3 * W_A + 2 * W_B
CONV_A_WIDTH = 31
CONV_B_WIDTH = 4
RG_C = 8.0
PLE_DIM = 256
EPS = 1e-6

SUBLANES = 8
LANES = 128
MXU_N = 256
VMEM_LIMIT_BYTES = 60000 * 1024

TT = 64
HIST_A = 32
HIST_B = 8
SLAB_PITCH = TT + 8
ROW_TILE = 16
CONV_T = 8
CONV_PAIRS = (CONV_A_WIDTH - 1) // 2
SAMPLE_BT = 16
CAST_STEPS = 8

NC_D = D_MODEL // MXU_N
NC_GLU = 2 * W_A // MXU_N
NC_REST = (W_IN - 2 * W_A) // MXU_N
CH_GATE_A = 0
CH_BX = W_A // MXU_N
CH_GATE_B = (W_A + W_B) // MXU_N

F32 = jnp.float32
BF16 = jnp.bfloat16


def _sigmoid(x):
    return 0.5 * jnp.tanh(0.5 * x) + 0.5


def _silu(x):
    hx = 0.5 * x
    return hx * jnp.tanh(hx) + hx


def _dot(a, b):
    return jnp.dot(a, b, preferred_element_type=F32)


def _rmsnorm(x, g):
    ms = jnp.mean(x * x, axis=-1, keepdims=True)
    return x * lax.rsqrt(ms + EPS) * g


def _layernorm(x, g, b):
    mu = jnp.mean(x, axis=-1, keepdims=True)
    xc = x - mu
    var = jnp.mean(xc * xc, axis=-1, keepdims=True)
    return xc * lax.rsqrt(var + EPS) * g + b


def _rglru_coeffs(r_pre, i_pre, xb, c_lam):
    r = _sigmoid(r_pre)
    ig = _sigmoid(i_pre)
    log_a = c_lam * r
    a = jnp.exp(log_a)
    th = jnp.tanh(log_a)
    y = -2.0 * th / (1.0 - th)
    root = jnp.where(y == 0.0, 0.0, y * lax.rsqrt(y))
    bx = root * (ig * xb)
    return a, bx


def _aligned(v, m):
    return v if isinstance(v, int) else pl.multiple_of(v, m)


def _chunk_cols(c):
    return slice(c * MXU_N, (c + 1) * MXU_N)


def _lanes_of_chunks(ref, first, rows_sl):
    return jnp.concatenate([ref[first + c, rows_sl, :] for c in range(NC_D)], axis=-1)


def _prompt_kernel(nb, x_hbm, p_hbm, gn_ref, win_ref, wdw_ref, bdw_ref, lng_ref, lnb_ref,
                   wcb_ref, bcb_ref, wri_ref, br_ref, bi_ref, lam_ref, wout_ref, wpe_ref,
                   wpg_ref, gf_ref,
                   y_hbm, na_ref, nbuf_ref, nh_ref,
                   u_s, z_s, zb_s, vp_s, cv_s, a_s, b_s, mixa_s, mixb_s, xb16_s, h_s, wb_s,
                   xs_s, ps_s, ys_s, xt_s, p16_s, sem_in, sem_out):
    rows = TT * nb
    ha = HIST_A * nb
    hb = HIST_B * nb
    step = pl.program_id(0)
    n_steps = pl.num_programs(0)
    n_tiles = rows // ROW_TILE
    cur = slice(hb, hb + rows)
    slot = step % 2

    def in_copies(s, sl):
        t0 = pl.multiple_of(s * TT, TT)
        cps = []
        for b in range(nb):
            dst_rows = pl.ds(b * SLAB_PITCH, TT)
            for j in range(D_MODEL // LANES):
                cps.append(pltpu.make_async_copy(
                    x_hbm.at[b, pl.ds(t0, TT), pl.ds(j * LANES, LANES)],
                    xs_s.at[sl, j, dst_rows, :], sem_in.at[sl]))
            for j in range(PLE_DIM // LANES):
                cps.append(pltpu.make_async_copy(
                    p_hbm.at[b, pl.ds(t0, TT), pl.ds(j * LANES, LANES)],
                    ps_s.at[sl, j, dst_rows, :], sem_in.at[sl]))
        return cps

    def out_copies(s):
        t0 = pl.multiple_of(s * TT, TT)
        return [pltpu.make_async_copy(
            ys_s.at[j, pl.ds(b * SLAB_PITCH, TT), :],
            y_hbm.at[b, pl.ds(t0, TT), pl.ds(j * LANES, LANES)], sem_out.at[0])
            for b in range(nb) for j in range(D_MODEL // LANES)]

    @pl.when(step == 0)
    def _first_fetch():
        for cp in in_copies(0, 0):
            cp.start()

    @pl.when(step + 1 < n_steps)
    def _prefetch():
        for cp in in_copies(step + 1, 1 - slot):
            cp.start()

    @pl.when(step == 0)
    def _init():
        vp_s[0:ha, :] = jnp.zeros((ha, W_A), F32)
        for c in range(NC_D):
            zb_s[CH_BX + c, 0:hb, :] = jnp.zeros((hb, MXU_N), F32)
        h_s[...] = jnp.zeros_like(h_s)
        for k in range(CONV_A_WIDTH):
            wb_s[k * nb:(k + 1) * nb, :] = jnp.broadcast_to(wdw_ref[k:k + 1, :], (nb, W_A))
        for i in range(CONV_PAIRS):
            k = CONV_A_WIDTH + i
            wb_s[k * nb:(k + 1) * nb, :] = jnp.broadcast_to(
                wdw_ref[2 * i:2 * i + 1, :] + wdw_ref[2 * i + 1:2 * i + 2, :], (nb, W_A))

    @pl.when(step > 0)
    def _carry():
        vp_s[0:ha, :] = vp_s[rows:rows + ha, :]
        for c in range(NC_D):
            zb_s[CH_BX + c, 0:hb, :] = zb_s[CH_BX + c, rows:rows + hb, :]

    def tiled_loop(n_iter, tile_body, host_body=None):
        per_iter = n_tiles // n_iter

        def body(it, c):
            if host_body is not None:
                host_body(it)
            for q in range(per_iter):
                tile_body(_aligned((it * per_iter + q) * ROW_TILE, ROW_TILE))
            return c
        if host_body is not None:
            for it in range(n_iter):
                body(it, 0)
        else:
            lax.fori_loop(0, n_iter, body, 0)

    for cp in in_copies(step, slot):
        cp.wait()

    def time_rows(slab, groups, r0):
        t = lax.shift_right_logical(r0, SUBLANES.bit_length() - 1)
        return jnp.concatenate(
            [jnp.concatenate([slab[slot, j, pl.ds(t + q, nb, stride=SLAB_PITCH), :] for j in range(groups)],
                             axis=-1) for q in range(ROW_TILE // nb)], axis=0)

    def norm_in(r0):
        rs = pl.ds(r0, ROW_TILE)
        x = time_rows(xs_s, D_MODEL // LANES, r0)
        xt_s[rs, :] = x
        u_s[rs, :] = _rmsnorm(x, gn_ref[...]).astype(BF16)
        p16_s[rs, :] = time_rows(ps_s, PLE_DIM // LANES, r0).astype(BF16)
    tiled_loop(2, norm_in)

    for c in range(NC_GLU):
        z_s[c] = _dot(u_s[...], win_ref[:, _chunk_cols(c)])

    def glu(r0):
        rs = pl.ds(r0, ROW_TILE)
        val = _lanes_of_chunks(z_s, 0, rs)
        gl = _lanes_of_chunks(z_s, NC_D, rs)
        vp_s[pl.ds(ha + r0, ROW_TILE), :] = val * _sigmoid(gl)
    tiled_loop(8, glu)

    half = CONV_T // 2
    p_handoff = {}

    def conv_piece(blk, j):
        base = blk * (CONV_T * nb)
        ls = slice(j * LANES, (j + 1) * LANES)
        xp_cache, tap_cache = {}, {}

        def xp(d):
            if d not in xp_cache:
                off = (d + HIST_A - (CONV_A_WIDTH - 1)) * nb
                xp_cache[d] = vp_s[base + off:base + off + nb, ls]
            return xp_cache[d]

        def tap(idx):
            if idx not in tap_cache:
                tap_cache[idx] = wb_s[idx * nb:(idx + 1) * nb, ls]
            return tap_cache[idx]

        def add(acc, term):
            return term if acc is None else acc + term

        p = [None] * (half + 1)
        q = [None] * half
        u = [None] * half
        if blk > 0:
            p[0] = p_handoff[j]
        for n in range(half + CONV_PAIRS):
            xe = xp(2 * n)
            last = n == half + CONV_PAIRS - 1
            xo = None if last else xp(2 * n + 1)
            xs = None if last else xo + xp(2 * n + 2)
            for m in range(half + 1):
                i = n - m
                if not 0 <= i < CONV_PAIRS:
                    continue
                if m > 0 or blk == 0:
                    p[m] = add(p[m], tap(2 * i) * xe)
                if m < half:
                    q[m] = add(q[m], tap(2 * i + 1) * xo)
                    u[m] = add(u[m], tap(CONV_A_WIDTH + i) * xs)
        p_handoff[j] = p[half]
        bias = jnp.broadcast_to(bdw_ref[:, ls], (nb, LANES))
        w_last = tap(CONV_A_WIDTH - 1)
        for m in range(half):
            even = (p[m] + q[m]) + (bias + w_last * xp(2 * m + CONV_A_WIDTH - 1))
            odd = (u[m] - p[m + 1] - q[m]) + (bias + w_last * xp(2 * m + CONV_A_WIDTH))
            cv_s[base + 2 * m * nb:base + (2 * m + 1) * nb, ls] = even
            cv_s[base + (2 * m + 1) * nb:base + (2 * m + 2) * nb, ls] = odd

    def in_dot(ch):
        zb_s[ch, cur, :] = _dot(u_s[...], win_ref[:, _chunk_cols(NC_GLU + ch)])

    pieces = [(blk, j) for j in range(W_A // LANES) for blk in range(TT // CONV_T)]
    next_dot = 0
    for n, piece in enumerate(pieces):
        want = -(-(n + 1) * NC_REST // len(pieces))
        while next_dot < want:
            in_dot(next_dot)
            next_dot += 1
        conv_piece(*piece)

    def conv_b(r0):
        rs = pl.ds(r0, ROW_TILE)
        for h in range(N_B_HEADS):
            ls = slice(h * HEAD_B, (h + 1) * HEAD_B)
            ch = CH_BX + (h * HEAD_B) // MXU_N
            lo = (h * HEAD_B) % MXU_N
            acc = jnp.broadcast_to(bcb_ref[:, ls], (ROW_TILE, HEAD_B))
            for k in range(CONV_B_WIDTH):
                off = (k + HIST_B - (CONV_B_WIDTH - 1)) * nb
                acc = acc + wcb_ref[k:k + 1, ls] * zb_s[ch, pl.ds(r0 + off, ROW_TILE), lo:lo + HEAD_B]
            a_s[rs, ls] = acc
            xb16_s[h, rs, :] = acc.astype(BF16)
    tiled_loop(8, conv_b)

    def gate_dot(it):
        for q in range(2):
            h = it * 2 + q
            z_s[h] = _dot(xb16_s[h], wri_ref[h])

    def mix_a(r0):
        rs = pl.ds(r0, ROW_TILE)
        v = _layernorm(cv_s[rs, :], lng_ref[...], lnb_ref[...])
        gate = _lanes_of_chunks(zb_s, CH_GATE_A, pl.ds(hb + r0, ROW_TILE))
        mixa_s[rs, :] = (_silu(v) * _silu(gate)).astype(BF16)
    tiled_loop(N_B_HEADS // 2, mix_a, gate_dot)

    c_lam = -RG_C * jax.nn.softplus(-lam_ref[...])

    def out_a_dot(c):
        zb_s[CH_GATE_A + c, cur, :] = _dot(mixa_s[...], wout_ref[0:W_A, _chunk_cols(c)])

    heads = [slice(h * HEAD_B, (h + 1) * HEAD_B) for h in range(N_B_HEADS)]
    h_run = [h_s[:, ls] for ls in heads]

    def coeffs(r0):
        rs = pl.ds(r0, ROW_TILE)
        for h, ls in enumerate(heads):
            r_pre = z_s[h, rs, 0:HEAD_B] + br_ref[:, ls]
            i_pre = z_s[h, rs, HEAD_B:2 * HEAD_B] + bi_ref[:, ls]
            a, bx = _rglru_coeffs(r_pre, i_pre, a_s[rs, ls], c_lam[:, ls])
            for q in range(ROW_TILE // nb):
                tq = slice(q * nb, (q + 1) * nb)
                h_run[h] = a[tq] * h_run[h] + bx[tq]
                b_s[pl.ds(r0 + q * nb, nb), ls] = h_run[h]
    tiled_loop(NC_D, coeffs, out_a_dot)
    for h, ls in enumerate(heads):
        h_s[:, ls] = h_run[h]

    def mix_b(r0):
        rs = pl.ds(r0, ROW_TILE)
        gate = _lanes_of_chunks(zb_s, CH_GATE_B, pl.ds(hb + r0, ROW_TILE))
        mixb_s[rs, :] = (b_s[rs, :] * _silu(gate)).astype(BF16)
    tiled_loop(8, mix_b)

    for c in range(NC_D):
        z_s[c] = _dot(mixb_s[...], wout_ref[W_A:W_A + W_B, _chunk_cols(c)])

    def resid(r0):
        rs = pl.ds(r0, ROW_TILE)
        h1 = (xt_s[rs, :] + _lanes_of_chunks(zb_s, CH_GATE_A, pl.ds(hb + r0, ROW_TILE))
              + _lanes_of_chunks(z_s, 0, rs))
        cv_s[rs, :] = h1
        u_s[rs, :] = h1.astype(BF16)
    tiled_loop(8, resid)

    for c in range(NC_D):
        z_s[c] = _dot(u_s[...], wpg_ref[:, _chunk_cols(c)])
        z_s[NC_D + c] = _dot(p16_s[...], wpe_ref[:, _chunk_cols(c)])

    @pl.when(step > 0)
    def _drain_prev():
        for cp in out_copies(step - 1):
            cp.wait()

    def finish(r0):
        rs = pl.ds(r0, ROW_TILE)
        pg = _lanes_of_chunks(z_s, 0, rs)
        pe = _lanes_of_chunks(z_s, NC_D, rs)
        y = _rmsnorm(cv_s[rs, :] + pe * _sigmoid(pg), gf_ref[...])
        t = lax.shift_right_logical(r0, SUBLANES.bit_length() - 1)
        for q in range(ROW_TILE // nb):
            for j in range(D_MODEL // LANES):
                ys_s[j, pl.ds(t + q, nb, stride=SLAB_PITCH), :] = y[q * nb:(q + 1) * nb, j * LANES:(j + 1) * LANES]
    tiled_loop(2, finish)

    for cp in out_copies(step):
        cp.start()

    @pl.when(step == n_steps - 1)
    def _drain_last():
        for cp in out_copies(step):
            cp.wait()

    @pl.when(step == n_steps - 1)
    def _state_out():
        na = (CONV_A_WIDTH - 1) * nb
        nbb = (CONV_B_WIDTH - 1) * nb
        na_ref[...] = vp_s[ha + rows - na:ha + rows, :]
        for c in range(NC_D):
            nbuf_ref[:, c * MXU_N:(c + 1) * MXU_N] = zb_s[CH_BX + c, hb + rows - nbb:hb + rows, :]
        nh_ref[...] = h_s[...]


def _sample_kernel(x_ref, p_ref, sa_ref, sb_ref, h0_ref, gn_ref, win_ref, wdw_ref, bdw_ref,
                   lng_ref, lnb_ref, wcb_ref, bcb_ref, wri_ref, br_ref, bi_ref, lam_ref,
                   wout_ref, wpe_ref, wpg_ref, gf_ref,
                   y_ref, na_ref, nbuf_ref, nh_ref,
                   u_s, z_s, v_s, cvp_s):
    step = pl.program_id(0)
    ka = CONV_A_WIDTH - 1
    kb = CONV_B_WIDTH - 1

    @pl.when(step == 0)
    def _project():
        u = _rmsnorm(x_ref[...], gn_ref[...]).astype(BF16)
        u_s[...] = u
        za = _dot(u, win_ref[:, 0:2 * W_A])
        v_s[...] = za[:, 0:W_A] * _sigmoid(za[:, W_A:2 * W_A])
        z_s[...] = _dot(u, win_ref[:, 2 * W_A:W_IN])

    seqs = pl.ds(pl.multiple_of(step * SAMPLE_BT, SAMPLE_BT), SAMPLE_BT)
    acc = wdw_ref[0:1, :] * sa_ref[0]
    for k in range(1, ka):
        acc = acc + wdw_ref[k:k + 1, :] * sa_ref[k]
        na_ref[k - 1] = sa_ref[k]
    cvp_s[seqs, :] = acc
    na_ref[ka - 1] = v_s[seqs, :]

    @pl.when(step == pl.num_programs(0) - 1)
    def _rest():
        v = v_s[...]
        cv = cvp_s[...] + wdw_ref[ka:ka + 1, :] * v + bdw_ref[...]
        v = _silu(_layernorm(cv, lng_ref[...], lnb_ref[...])) * _silu(z_s[:, 0:W_A])
        mix_a = v.astype(BF16)

        b_x = z_s[:, W_A:W_A + W_B]
        xb = wcb_ref[kb:kb + 1, :] * b_x + bcb_ref[...]
        for k in range(kb):
            xb = xb + wcb_ref[k:k + 1, :] * sb_ref[k]
        for k in range(kb - 1):
            nbuf_ref[k] = sb_ref[k + 1]
        nbuf_ref[kb - 1] = b_x

        xb16 = xb.astype(BF16)
        c_lam = -RG_C * jax.nn.softplus(-lam_ref[...])
        a_parts, b_parts = [], []
        for h in range(N_B_HEADS):
            ls = slice(h * HEAD_B, (h + 1) * HEAD_B)
            g = _dot(xb16[:, ls], wri_ref[h])
            a, bx = _rglru_coeffs(g[:, 0:HEAD_B] + br_ref[:, ls], g[:, HEAD_B:2 * HEAD_B] + bi_ref[:, ls],
                                  xb[:, ls], c_lam[:, ls])
            a_parts.append(a)
            b_parts.append(bx)
        hn = jnp.concatenate(a_parts, axis=-1) * h0_ref[...] + jnp.concatenate(b_parts, axis=-1)
        nh_ref[...] = hn
        mix_b = (hn * _silu(z_s[:, W_A + W_B:W_A + 2 * W_B])).astype(BF16)

        mix = _dot(mix_a, wout_ref[0:W_A, :]) + _dot(mix_b, wout_ref[W_A:W_A + W_B, :])
        h1 = x_ref[...] + mix
        pe = _dot(p_ref[...].astype(BF16), wpe_ref[...]) * _sigmoid(_dot(h1.astype(BF16), wpg_ref[...]))
        y_ref[...] = _rmsnorm(h1 + pe, gf_ref[...])


def _cast_kernel(*refs):
    n = len(refs) // 2
    for src, dst in zip(refs[:n], refs[n:]):
        dst[...] = src[...].astype(BF16)


def _cast_weights(*ws):
    spec = lambda w: pl.BlockSpec((w.shape[0] // CAST_STEPS, w.shape[1]), lambda i: (i, 0))
    return pl.pallas_call(
        _cast_kernel,
        grid=(CAST_STEPS,),
        in_specs=[spec(w) for w in ws],
        out_specs=[spec(w) for w in ws],
        out_shape=[jax.ShapeDtypeStruct(w.shape, BF16) for w in ws],
        compiler_params=pltpu.CompilerParams(dimension_semantics=("arbitrary",)),
        name="cast_weights",
    )(*ws)


def _const_spec(shape):
    zeros = (0,) * len(shape)
    return pl.BlockSpec(shape, lambda i: zeros, pipeline_mode=pl.Buffered(1))


def _weight_specs():
    return [
        _const_spec((1, D_MODEL)),
        _const_spec((D_MODEL, W_IN)),
        _const_spec((CONV_A_WIDTH, W_A)),
        _const_spec((1, W_A)),
        _const_spec((1, W_A)),
        _const_spec((1, W_A)),
        _const_spec((CONV_B_WIDTH, W_B)),
        _const_spec((1, W_B)),
        _const_spec((N_B_HEADS, HEAD_B, 2 * HEAD_B)),
        _const_spec((1, W_B)),
        _const_spec((1, W_B)),
        _const_spec((1, W_B)),
        _const_spec((W_A + W_B, D_MODEL)),
        _const_spec((PLE_DIM, D_MODEL)),
        _const_spec((D_MODEL, D_MODEL)),
        _const_spec((1, D_MODEL)),
    ]


def _prompt_call(x, p, weights):
    nb, seq, _ = x.shape
    rows = TT * nb
    n_a = (CONV_A_WIDTH - 1) * nb
    n_b = (CONV_B_WIDTH - 1) * nb
    slab_rows = nb * SLAB_PITCH
    hbm = pl.BlockSpec(memory_space=pl.ANY)
    out_shape = (
        jax.ShapeDtypeStruct((nb, seq, D_MODEL), F32),
        jax.ShapeDtypeStruct((n_a, W_A), F32),
        jax.ShapeDtypeStruct((n_b, W_B), F32),
        jax.ShapeDtypeStruct((nb, W_B), F32),
    )
    const_out = lambda shape: pl.BlockSpec(shape, lambda i: (0, 0))
    return pl.pallas_call(
        functools.partial(_prompt_kernel, nb),
        grid=(seq // TT,),
        in_specs=[hbm, hbm] + _weight_specs(),
        out_specs=(hbm, const_out((n_a, W_A)), const_out((n_b, W_B)), const_out((nb, W_B))),
        out_shape=out_shape,
        scratch_shapes=[
            pltpu.VMEM((rows, D_MODEL), BF16),
            pltpu.VMEM((NC_GLU, rows, MXU_N), F32),
            pltpu.VMEM((NC_REST, HIST_B * nb + rows, MXU_N), F32),
            pltpu.VMEM(((HIST_A + TT) * nb, W_A), F32),
            pltpu.VMEM((rows, D_MODEL), F32),
            pltpu.VMEM((rows, W_B), F32),
            pltpu.VMEM((rows, W_B), F32),
            pltpu.VMEM((rows, W_A), BF16),
            pltpu.VMEM((rows, W_B), BF16),
            pltpu.VMEM((N_B_HEADS, rows, HEAD_B), BF16),
            pltpu.VMEM((nb, W_B), F32),
            pltpu.VMEM(((CONV_A_WIDTH + CONV_PAIRS) * nb, W_A), F32),
            pltpu.VMEM((2, D_MODEL // LANES, slab_rows, LANES), F32),
            pltpu.VMEM((2, PLE_DIM // LANES, slab_rows, LANES), F32),
            pltpu.VMEM((D_MODEL // LANES, slab_rows, LANES), F32),
            pltpu.VMEM((rows, D_MODEL), F32),
            pltpu.VMEM((rows, PLE_DIM), BF16),
            pltpu.SemaphoreType.DMA((2,)),
            pltpu.SemaphoreType.DMA((1,)),
        ],
        compiler_params=pltpu.CompilerParams(
            dimension_semantics=("arbitrary",), vmem_limit_bytes=VMEM_LIMIT_BYTES),
        name="prompt_layer",
    )(x, p, *weights)


def _sample_call(x_s, p_s, sa, sb_t, h0, weights):
    n = x_s.shape[0]
    ka = CONV_A_WIDTH - 1
    kb = CONV_B_WIDTH - 1
    out_shape = (
        jax.ShapeDtypeStruct((n, D_MODEL), F32),
        jax.ShapeDtypeStruct((ka, n, W_A), F32),
        jax.ShapeDtypeStruct((kb, n, W_B), F32),
        jax.ShapeDtypeStruct((n, W_B), F32),
    )
    full2 = lambda shape: pl.BlockSpec(shape, lambda i: (0, 0))
    full3 = lambda shape: pl.BlockSpec(shape, lambda i: (0, 0, 0))
    return pl.pallas_call(
        _sample_kernel,
        grid=(n // SAMPLE_BT,),
        in_specs=[full2((n, D_MODEL)), full2((n, PLE_DIM)),
                  pl.BlockSpec((ka, SAMPLE_BT, W_A), lambda i: (0, i, 0)),
                  full3((kb, n, W_B)), full2((n, W_B))] + _weight_specs(),
        out_specs=(full2((n, D_MODEL)),
                   pl.BlockSpec((ka, SAMPLE_BT, W_A), lambda i: (0, i, 0)),
                   full3((kb, n, W_B)), full2((n, W_B))),
        out_shape=out_shape,
        scratch_shapes=[
            pltpu.VMEM((n, D_MODEL), BF16),
            pltpu.VMEM((n, W_IN - 2 * W_A), F32),
            pltpu.VMEM((n, W_A), F32),
            pltpu.VMEM((n, W_A), F32),
        ],
        compiler_params=pltpu.CompilerParams(
            dimension_semantics=("arbitrary",), vmem_limit_bytes=VMEM_LIMIT_BYTES),
        name="sample_layer",
    )(x_s, p_s, sa, sb_t, h0, *weights)


def kernel(x_prompt, x_sample, p_prompt, p_sample, state_conv_a, state_conv_b, state_h, g_norm, w_in, w_dw_a, b_dw_a, ln_g, ln_b, w_conv_b, b_conv_b, w_r, b_r, w_i, b_i, lam, w_out, w_pe, w_pg, g_final):
    depth = g_norm.shape[0]
    assert depth == 1, "single-layer step"
    nb, seq, _ = x_prompt.shape
    assert nb == SUBLANES and seq % TT == 0 and TT >= HIST_A
    row = lambda a: a.reshape(1, -1)
    win16, wout16, wpe16, wpg16 = _cast_weights(w_in[0], w_out[0], w_pe[0], w_pg[0])
    weights = (
        row(g_norm[0]), win16,
        w_dw_a[0], row(b_dw_a[0]), row(ln_g[0]), row(ln_b[0]),
        w_conv_b[0], row(b_conv_b[0]),
        jnp.concatenate([w_r[0], w_i[0]], axis=-1).astype(BF16), row(b_r[0]), row(b_i[0]), row(lam[0]),
        wout16, wpe16, wpg16, row(g_final),
    )

    y_prompt, na_t, nbuf_t, nh_p = _prompt_call(x_prompt, p_prompt[0], weights)
    untime = lambda a, k: jnp.swapaxes(a.reshape(k, nb, a.shape[-1]), 0, 1)
    na_p = untime(na_t, CONV_A_WIDTH - 1)[None]
    nb_p = untime(nbuf_t, CONV_B_WIDTH - 1)[None]

    n_s = x_sample.shape[0]
    assert x_sample.shape[1] == 1 and n_s % SAMPLE_BT == 0
    y_s, na_s, nbuf_s, nh_s = _sample_call(
        x_sample.reshape(n_s, D_MODEL), p_sample[0].reshape(n_s, PLE_DIM), jnp.swapaxes(state_conv_a[0], 0, 1),
        jnp.swapaxes(state_conv_b[0], 0, 1), state_h[0], weights)
    return (y_prompt, y_s.reshape(n_s, 1, D_MODEL), na_p, nb_p, nh_p[None],
            jnp.swapaxes(na_s, 0, 1)[None], jnp.swapaxes(nbuf_s, 0, 1)[None], nh_s[None])
```

```python
import functools

import jax
import jax.numpy as jnp
from jax import lax
from jax.experimental import pallas as pl
from jax.experimental.pallas import tpu as pltpu

D_MODEL = 1024
W_A = D_MODEL
W_B = D_MODEL
N_B_HEADS = 8
HEAD_B = W_B // N_B_HEADS
W_IN = 3 * W_A + 2 * W_B
CONV_A_WIDTH = 31
CONV_B_WIDTH = 4
RG_C = 8.0
PLE_DIM = 256
EPS = 1e-6

SUBLANES = 8
LANES = 128
MXU_N = 256
VMEM_LIMIT_BYTES = 60000 * 1024

TT = 64
HIST_A = 32
HIST_B = 4
SLAB_PITCH = TT + 8
ROW_TILE = 16
CONV_T = 8
CONV_PAIRS = (CONV_A_WIDTH - 1) // 2
SAMPLE_BT = 16
CAST_STEPS = 8

NC_D = D_MODEL // MXU_N
NC_GLU = 2 * W_A // MXU_N
NC_REST = (W_IN - 2 * W_A) // MXU_N
CH_GATE_A = 0
CH_BX = W_A // MXU_N
CH_GATE_B = (W_A + W_B) // MXU_N

F32 = jnp.float32
BF16 = jnp.bfloat16


def _sigmoid(x):
    return 0.5 * jnp.tanh(0.5 * x) + 0.5


def _silu(x):
    hx = 0.5 * x
    return hx * jnp.tanh(hx) + hx


def _dot(a, b):
    return jnp.dot(a, b, preferred_element_type=F32)


def _rmsnorm(x, g):
    ms = jnp.mean(x * x, axis=-1, keepdims=True)
    return x * lax.rsqrt(ms + EPS) * g


def _layernorm(x, g, b):
    mu = jnp.mean(x, axis=-1, keepdims=True)
    xc = x - mu
    var = jnp.mean(xc * xc, axis=-1, keepdims=True)
    return xc * lax.rsqrt(var + EPS) * g + b


def _rglru_coeffs(r, ig, xb, c_lam):
    log_a = c_lam * r
    a = jnp.exp(log_a)
    th = jnp.tanh(log_a)
    y = -2.0 * th / (1.0 - th)
    root = jnp.where(y == 0.0, 0.0, y * lax.rsqrt(y))
    bx = root * (ig * xb)
    return a, bx


def _aligned(v, m):
    return v if isinstance(v, int) else pl.multiple_of(v, m)


def _chunk_cols(c):
    return slice(c * MXU_N, (c + 1) * MXU_N)


def _lanes_of_chunks(ref, first, rows_sl):
    return jnp.concatenate([ref[first + c, rows_sl, :] for c in range(NC_D)], axis=-1)


def _prompt_kernel(nb, x_hbm, p_hbm, gn_ref, win_ref, wdw_ref, bdw_ref, lng_ref, lnb_ref,
                   wcb_ref, bcb_ref, wri_ref, br_ref, bi_ref, lam_ref, wout_ref, wpe_ref,
                   wpg_ref, gf_ref,
                   y_hbm, na_ref, nbuf_ref, nh_ref,
                   u_s, z_s, zb_s, vp_s, cv_s, a_s, mixa_s, mixb_s, xb16_s, h_s, wb_s,
                   xs_s, ps_s, ys_s, p16_s, sem_in, sem_out):
    rows = TT * nb
    ha = HIST_A * nb
    hb = HIST_B * nb
    step = pl.program_id(0)
    n_steps = pl.num_programs(0)
    n_tiles = rows // ROW_TILE
    cur = slice(hb, hb + rows)
    heads = [slice(h * HEAD_B, (h + 1) * HEAD_B) for h in range(N_B_HEADS)]
    slot = step % 2

    def in_copies(s, sl):
        t0 = pl.multiple_of(s * TT, TT)
        cps = []
        for b in range(nb):
            dst_rows = pl.ds(b * SLAB_PITCH, TT)
            for j in range(D_MODEL // LANES):
                cps.append(pltpu.make_async_copy(
                    x_hbm.at[b, pl.ds(t0, TT), pl.ds(j * LANES, LANES)],
                    xs_s.at[sl, j, dst_rows, :], sem_in.at[sl]))
            for j in range(PLE_DIM // LANES):
                cps.append(pltpu.make_async_copy(
                    p_hbm.at[b, pl.ds(t0, TT), pl.ds(j * LANES, LANES)],
                    ps_s.at[sl, j, dst_rows, :], sem_in.at[sl]))
        return cps

    def out_copies(s):
        t0 = pl.multiple_of(s * TT, TT)
        return [pltpu.make_async_copy(
            ys_s.at[j, pl.ds(b * SLAB_PITCH, TT), :],
            y_hbm.at[b, pl.ds(t0, TT), pl.ds(j * LANES, LANES)], sem_out.at[0])
            for b in range(nb) for j in range(D_MODEL // LANES)]

    @pl.when(step == 0)
    def _first_fetch():
        for cp in in_copies(0, 0):
            cp.start()

    @pl.when(step + 1 < n_steps)
    def _prefetch():
        for cp in in_copies(step + 1, 1 - slot):
            cp.start()

    @pl.when(step == 0)
    def _init():
        vp_s[0:ha, :] = jnp.zeros((ha, W_A), F32)
        for c in range(NC_D):
            zb_s[CH_BX + c, 0:hb, :] = jnp.zeros((hb, MXU_N), F32)
        h_s[...] = jnp.zeros_like(h_s)
        for k in range(CONV_A_WIDTH):
            wb_s[k * nb:(k + 1) * nb, :] = jnp.broadcast_to(wdw_ref[k:k + 1, :], (nb, W_A))
        for i in range(CONV_PAIRS):
            k = CONV_A_WIDTH + i
            wb_s[k * nb:(k + 1) * nb, :] = jnp.broadcast_to(
                wdw_ref[2 * i:2 * i + 1, :] + wdw_ref[2 * i + 1:2 * i + 2, :], (nb, W_A))

    @pl.when(step > 0)
    def _carry():
        vp_s[0:ha, :] = vp_s[rows:rows + ha, :]
        for c in range(NC_D):
            zb_s[CH_BX + c, 0:hb, :] = zb_s[CH_BX + c, rows:rows + hb, :]

    def tiled_loop(n_iter, tile_body, host_body=None):
        per_iter = n_tiles // n_iter

        def body(it, c):
            if host_body is not None:
                host_body(it)
            for q in range(per_iter):
                tile_body(_aligned((it * per_iter + q) * ROW_TILE, ROW_TILE))
            return c
        if host_body is not None:
            for it in range(n_iter):
                body(it, 0)
        else:
            lax.fori_loop(0, n_iter, body, 0)

    for cp in in_copies(step, slot):
        cp.wait()

    def time_rows(slab, groups, r0):
        t = lax.shift_right_logical(r0, SUBLANES.bit_length() - 1)
        return jnp.concatenate(
            [jnp.concatenate([slab[slot, j, pl.ds(t + q, nb, stride=SLAB_PITCH), :] for j in range(groups)],
                             axis=-1) for q in range(ROW_TILE // nb)], axis=0)

    def norm_in(r0):
        rs = pl.ds(r0, ROW_TILE)
        x = time_rows(xs_s, D_MODEL // LANES, r0)
        u_s[rs, :] = _rmsnorm(x, gn_ref[...]).astype(BF16)
        p16_s[rs, :] = time_rows(ps_s, PLE_DIM // LANES, r0).astype(BF16)
    tiled_loop(2, norm_in)

    for c in range(NC_D):
        z_s[c] = _dot(u_s[...], win_ref[:, _chunk_cols(c)])
        z_s[NC_D + c] = _sigmoid(_dot(u_s[...], win_ref[:, _chunk_cols(NC_D + c)]))

    def glu(r0):
        rs = pl.ds(r0, ROW_TILE)
        vp_s[pl.ds(ha + r0, ROW_TILE), :] = _lanes_of_chunks(z_s, 0, rs) * _lanes_of_chunks(z_s, NC_D, rs)
    tiled_loop(8, glu)

    half = CONV_T // 2
    p_handoff = {}

    def conv_piece(blk, j):
        base = blk * (CONV_T * nb)
        ls = slice(j * LANES, (j + 1) * LANES)
        xp_cache, tap_cache = {}, {}

        def xp(d):
            if d not in xp_cache:
                off = (d + HIST_A - (CONV_A_WIDTH - 1)) * nb
                xp_cache[d] = vp_s[base + off:base + off + nb, ls]
            return xp_cache[d]

        def tap(idx):
            if idx not in tap_cache:
                tap_cache[idx] = wb_s[idx * nb:(idx + 1) * nb, ls]
            return tap_cache[idx]

        def add(acc, term):
            return term if acc is None else acc + term

        p = [None] * (half + 1)
        q = [None] * half
        u = [None] * half
        if blk > 0:
            p[0] = p_handoff[j]
        for n in range(half + CONV_PAIRS):
            xe = xp(2 * n)
            last = n == half + CONV_PAIRS - 1
            xo = None if last else xp(2 * n + 1)
            xs = None if last else xo + xp(2 * n + 2)
            for m in range(half + 1):
                i = n - m
                if not 0 <= i < CONV_PAIRS:
                    continue
                if m > 0 or blk == 0:
                    p[m] = add(p[m], tap(2 * i) * xe)
                if m < half:
                    q[m] = add(q[m], tap(2 * i + 1) * xo)
                    u[m] = add(u[m], tap(CONV_A_WIDTH + i) * xs)
        p_handoff[j] = p[half]
        bias = jnp.broadcast_to(bdw_ref[:, ls], (nb, LANES))
        w_last = tap(CONV_A_WIDTH - 1)
        for m in range(half):
            even = (p[m] + q[m]) + (bias + w_last * xp(2 * m + CONV_A_WIDTH - 1))
            odd = (u[m] - p[m + 1] - q[m]) + (bias + w_last * xp(2 * m + CONV_A_WIDTH))
            cv_s[base + 2 * m * nb:base + (2 * m + 1) * nb, ls] = even
            cv_s[base + (2 * m + 1) * nb:base + (2 * m + 2) * nb, ls] = odd

    def in_dot(ch):
        z = _dot(u_s[...], win_ref[:, _chunk_cols(NC_GLU + ch)])
        is_gate = not CH_BX <= ch < CH_GATE_B
        zb_s[ch, cur, :] = _silu(z) if is_gate else z

    pieces = [(blk, j) for j in range(W_A // LANES) for blk in range(TT // CONV_T)]
    next_dot = 0
    for n, piece in enumerate(pieces):
        want = -(-(n + 1) * NC_REST // len(pieces))
        while next_dot < want:
            in_dot(next_dot)
            next_dot += 1
        conv_piece(*piece)

    def conv_b(r0):
        rs = pl.ds(r0, ROW_TILE)
        for h in range(N_B_HEADS):
            ls = slice(h * HEAD_B, (h + 1) * HEAD_B)
            ch = CH_BX + (h * HEAD_B) // MXU_N
            lo = (h * HEAD_B) % MXU_N
            acc = jnp.broadcast_to(bcb_ref[:, ls], (ROW_TILE, HEAD_B))
            for k in range(CONV_B_WIDTH):
                off = (k + HIST_B - (CONV_B_WIDTH - 1)) * nb
                acc = acc + wcb_ref[k:k + 1, ls] * zb_s[ch, pl.ds(r0 + off, ROW_TILE), lo:lo + HEAD_B]
            a_s[rs, ls] = acc
            xb16_s[h, rs, :] = acc.astype(BF16)
    tiled_loop(8, conv_b)

    def gate_dot(it):
        for q in range(2):
            h = it * 2 + q
            bias = jnp.concatenate([br_ref[:, heads[h]], bi_ref[:, heads[h]]], axis=-1)
            z_s[h] = _sigmoid(_dot(xb16_s[h], wri_ref[h]) + bias)

    def mix_a(r0):
        rs = pl.ds(r0, ROW_TILE)
        v = _layernorm(cv_s[rs, :], lng_ref[...], lnb_ref[...])
        gate = _lanes_of_chunks(zb_s, CH_GATE_A, pl.ds(hb + r0, ROW_TILE))
        mixa_s[rs, :] = (_silu(v) * gate).astype(BF16)
    tiled_loop(N_B_HEADS // 2, mix_a, gate_dot)

    c_lam = -RG_C * jax.nn.softplus(-lam_ref[...])

    def out_a_dot(c):
        zb_s[CH_GATE_A + c, cur, :] = _dot(mixa_s[...], wout_ref[0:W_A, _chunk_cols(c)])

    h_run = [h_s[:, ls] for ls in heads]

    def coeffs(r0):
        rs = pl.ds(r0, ROW_TILE)
        for h, ls in enumerate(heads):
            a, bx = _rglru_coeffs(z_s[h, rs, 0:HEAD_B], z_s[h, rs, HEAD_B:2 * HEAD_B], a_s[rs, ls], c_lam[:, ls])
            hs = []
            for q in range(ROW_TILE // nb):
                tq = slice(q * nb, (q + 1) * nb)
                h_run[h] = a[tq] * h_run[h] + bx[tq]
                hs.append(h_run[h])
            ch, lo = CH_GATE_B + (h * HEAD_B) // MXU_N, (h * HEAD_B) % MXU_N
            gate = zb_s[ch, hb + r0:hb + r0 + ROW_TILE, lo:lo + HEAD_B]
            mixb_s[rs, ls] = (jnp.concatenate(hs, axis=0) * gate).astype(BF16)
    tiled_loop(NC_D, coeffs, out_a_dot)
    for h, ls in enumerate(heads):
        h_s[:, ls] = h_run[h]

    def x_cols(c):
        lanes = []
        for j in range(c * (MXU_N // LANES), (c + 1) * (MXU_N // LANES)):
            lanes.append(jnp.concatenate(
                [xs_s[slot, j, pl.ds(t, nb, stride=SLAB_PITCH), :] for t in range(TT)], axis=0))
        return jnp.concatenate(lanes, axis=-1)

    for c in range(NC_D):
        h1 = (_dot(mixb_s[...], wout_ref[W_A:W_A + W_B, _chunk_cols(c)])
              + zb_s[CH_GATE_A + c, cur, :] + x_cols(c))
        cv_s[:, _chunk_cols(c)] = h1
        u_s[:, _chunk_cols(c)] = h1.astype(BF16)

    for c in range(NC_D):
        z_s[c] = _sigmoid(_dot(u_s[...], wpg_ref[:, _chunk_cols(c)]))
        cv_s[:, _chunk_cols(c)] = cv_s[:, _chunk_cols(c)] + _dot(p16_s[...], wpe_ref[:, _chunk_cols(c)]) * z_s[c]

    @pl.when(step > 0)
    def _drain_prev():
        for cp in out_copies(step - 1):
            cp.wait()

    def finish(r0):
        rs = pl.ds(r0, ROW_TILE)
        y = _rmsnorm(cv_s[rs, :], gf_ref[...])
        t = lax.shift_right_logical(r0, SUBLANES.bit_length() - 1)
        for q in range(ROW_TILE // nb):
            for j in range(D_MODEL // LANES):
                ys_s[j, pl.ds(t + q, nb, stride=SLAB_PITCH), :] = y[q * nb:(q + 1) * nb, j * LANES:(j + 1) * LANES]
    tiled_loop(2, finish)

    for cp in out_copies(step):
        cp.start()

    @pl.when(step == n_steps - 1)
    def _drain_last():
        for cp in out_copies(step):
            cp.wait()

    @pl.when(step == n_steps - 1)
    def _state_out():
        na = (CONV_A_WIDTH - 1) * nb
        nbb = (CONV_B_WIDTH - 1) * nb
        na_ref[...] = vp_s[ha + rows - na:ha + rows, :]
        for c in range(NC_D):
            nbuf_ref[:, c * MXU_N:(c + 1) * MXU_N] = zb_s[CH_BX + c, hb + rows - nbb:hb + rows, :]
        nh_ref[...] = h_s[...]


def _sample_kernel(x_ref, p_ref, sa_ref, sb_ref, h0_ref, gn_ref, win_ref, wdw_ref, bdw_ref,
                   lng_ref, lnb_ref, wcb_ref, bcb_ref, wri_ref, br_ref, bi_ref, lam_ref,
                   wout_ref, wpe_ref, wpg_ref, gf_ref,
                   y_ref, na_ref, nbuf_ref, nh_ref,
                   u_s, z_s, v_s, cvp_s):
    step = pl.program_id(0)
    ka = CONV_A_WIDTH - 1
    kb = CONV_B_WIDTH - 1

    @pl.when(step == 0)
    def _project():
        u = _rmsnorm(x_ref[...], gn_ref[...]).astype(BF16)
        u_s[...] = u
        za = _dot(u, win_ref[:, 0:2 * W_A])
        v_s[...] = za[:, 0:W_A] * _sigmoid(za[:, W_A:2 * W_A])
        z_s[...] = _dot(u, win_ref[:, 2 * W_A:W_IN])

    seqs = pl.ds(pl.multiple_of(step * SAMPLE_BT, SAMPLE_BT), SAMPLE_BT)
    acc = wdw_ref[0:1, :] * sa_ref[0]
    for k in range(1, ka):
        acc = acc + wdw_ref[k:k + 1, :] * sa_ref[k]
        na_ref[k - 1] = sa_ref[k]
    cvp_s[seqs, :] = acc
    na_ref[ka - 1] = v_s[seqs, :]

    @pl.when(step == pl.num_programs(0) - 1)
    def _rest():
        v = v_s[...]
        cv = cvp_s[...] + wdw_ref[ka:ka + 1, :] * v + bdw_ref[...]
        v = _silu(_layernorm(cv, lng_ref[...], lnb_ref[...])) * _silu(z_s[:, 0:W_A])
        mix_a = v.astype(BF16)

        b_x = z_s[:, W_A:W_A + W_B]
        xb = wcb_ref[kb:kb + 1, :] * b_x + bcb_ref[...]
        for k in range(kb):
            xb = xb + wcb_ref[k:k + 1, :] * sb_ref[k]
        for k in range(kb - 1):
            nbuf_ref[k] = sb_ref[k + 1]
        nbuf_ref[kb - 1] = b_x

        xb16 = xb.astype(BF16)
        c_lam = -RG_C * jax.nn.softplus(-lam_ref[...])
        a_parts, b_parts = [], []
        for h in range(N_B_HEADS):
            ls = slice(h * HEAD_B, (h + 1) * HEAD_B)
            g = _dot(xb16[:, ls], wri_ref[h])
            a, bx = _rglru_coeffs(_sigmoid(g[:, 0:HEAD_B] + br_ref[:, ls]),
                                  _sigmoid(g[:, HEAD_B:2 * HEAD_B] + bi_ref[:, ls]), xb[:, ls], c_lam[:, ls])
            a_parts.append(a)
            b_parts.append(bx)
        hn = jnp.concatenate(a_parts, axis=-1) * h0_ref[...] + jnp.concatenate(b_parts, axis=-1)
        nh_ref[...] = hn
        mix_b = (hn * _silu(z_s[:, W_A + W_B:W_A + 2 * W_B])).astype(BF16)

        mix = _dot(mix_a, wout_ref[0:W_A, :]) + _dot(mix_b, wout_ref[W_A:W_A + W_B, :])
        h1 = x_ref[...] + mix
        pe = _dot(p_ref[...].astype(BF16), wpe_ref[...]) * _sigmoid(_dot(h1.astype(BF16), wpg_ref[...]))
        y_ref[...] = _rmsnorm(h1 + pe, gf_ref[...])


def _cast_kernel(*refs):
    n = len(refs) // 2
    for src, dst in zip(refs[:n], refs[n:]):
        dst[...] = src[...].astype(BF16)


def _cast_weights(*ws):
    spec = lambda w: pl.BlockSpec((w.shape[0] // CAST_STEPS, w.shape[1]), lambda i: (i, 0))
    return pl.pallas_call(
        _cast_kernel,
        grid=(CAST_STEPS,),
        in_specs=[spec(w) for w in ws],
        out_specs=[spec(w) for w in ws],
        out_shape=[jax.ShapeDtypeStruct(w.shape, BF16) for w in ws],
        compiler_params=pltpu.CompilerParams(dimension_semantics=("arbitrary",)),
        name="cast_weights",
    )(*ws)


def _const_spec(shape):
    zeros = (0,) * len(shape)
    return pl.BlockSpec(shape, lambda i: zeros, pipeline_mode=pl.Buffered(1))


def _weight_specs():
    return [
        _const_spec((1, D_MODEL)),
        _const_spec((D_MODEL, W_IN)),
        _const_spec((CONV_A_WIDTH, W_A)),
        _const_spec((1, W_A)),
        _const_spec((1, W_A)),
        _const_spec((1, W_A)),
        _const_spec((CONV_B_WIDTH, W_B)),
        _const_spec((1, W_B)),
        _const_spec((N_B_HEADS, HEAD_B, 2 * HEAD_B)),
        _const_spec((1, W_B)),
        _const_spec((1, W_B)),
        _const_spec((1, W_B)),
        _const_spec((W_A + W_B, D_MODEL)),
        _const_spec((PLE_DIM, D_MODEL)),
        _const_spec((D_MODEL, D_MODEL)),
        _const_spec((1, D_MODEL)),
    ]


def _prompt_call(x, p, weights):
    nb, seq, _ = x.shape
    rows = TT * nb
    n_a = (CONV_A_WIDTH - 1) * nb
    n_b = (CONV_B_WIDTH - 1) * nb
    slab_rows = nb * SLAB_PITCH
    hbm = pl.BlockSpec(memory_space=pl.ANY)
    out_shape = (
        jax.ShapeDtypeStruct((nb, seq, D_MODEL), F32),
        jax.ShapeDtypeStruct((n_a, W_A), F32),
        jax.ShapeDtypeStruct((n_b, W_B), F32),
        jax.ShapeDtypeStruct((nb, W_B), F32),
    )
    const_out = lambda shape: pl.BlockSpec(shape, lambda i: (0, 0))
    return pl.pallas_call(
        functools.partial(_prompt_kernel, nb),
        grid=(seq // TT,),
        in_specs=[hbm, hbm] + _weight_specs(),
        out_specs=(hbm, const_out((n_a, W_A)), const_out((n_b, W_B)), const_out((nb, W_B))),
        out_shape=out_shape,
        scratch_shapes=[
            pltpu.VMEM((rows, D_MODEL), BF16),
            pltpu.VMEM((NC_GLU, rows, MXU_N), F32),
            pltpu.VMEM((NC_REST, HIST_B * nb + rows, MXU_N), F32),
            pltpu.VMEM(((HIST_A + TT) * nb, W_A), F32),
            pltpu.VMEM((rows, D_MODEL), F32),
            pltpu.VMEM((rows, W_B), F32),
            pltpu.VMEM((rows, W_A), BF16),
            pltpu.VMEM((rows, W_B), BF16),
            pltpu.VMEM((N_B_HEADS, rows, HEAD_B), BF16),
            pltpu.VMEM((nb, W_B), F32),
            pltpu.VMEM(((CONV_A_WIDTH + CONV_PAIRS) * nb, W_A), F32),
            pltpu.VMEM((2, D_MODEL // LANES, slab_rows, LANES), F32),
            pltpu.VMEM((2, PLE_DIM // LANES, slab_rows, LANES), F32),
            pltpu.VMEM((D_MODEL // LANES, slab_rows, LANES), F32),
            pltpu.VMEM((rows, PLE_DIM), BF16),
            pltpu.SemaphoreType.DMA((2,)),
            pltpu.SemaphoreType.DMA((1,)),
        ],
        compiler_params=pltpu.CompilerParams(
            dimension_semantics=("arbitrary",), vmem_limit_bytes=VMEM_LIMIT_BYTES),
        name="prompt_layer",
    )(x, p, *weights)


def _sample_call(x_s, p_s, sa, sb_t, h0, weights):
    n = x_s.shape[0]
    ka = CONV_A_WIDTH - 1
    kb = CONV_B_WIDTH - 1
    out_shape = (
        jax.ShapeDtypeStruct((n, D_MODEL), F32),
        jax.ShapeDtypeStruct((ka, n, W_A), F32),
        jax.ShapeDtypeStruct((kb, n, W_B), F32),
        jax.ShapeDtypeStruct((n, W_B), F32),
    )
    full2 = lambda shape: pl.BlockSpec(shape, lambda i: (0, 0))
    full3 = lambda shape: pl.BlockSpec(shape, lambda i: (0, 0, 0))
    return pl.pallas_call(
        _sample_kernel,
        grid=(n // SAMPLE_BT,),
        in_specs=[full2((n, D_MODEL)), full2((n, PLE_DIM)),
                  pl.BlockSpec((ka, SAMPLE_BT, W_A), lambda i: (0, i, 0)),
                  full3((kb, n, W_B)), full2((n, W_B))] + _weight_specs(),
        out_specs=(full2((n, D_MODEL)),
                   pl.BlockSpec((ka, SAMPLE_BT, W_A), lambda i: (0, i, 0)),
                   full3((kb, n, W_B)), full2((n, W_B))),
        out_shape=out_shape,
        scratch_shapes=[
            pltpu.VMEM((n, D_MODEL), BF16),
            pltpu.VMEM((n, W_IN - 2 * W_A), F32),
            pltpu.VMEM((n, W_A), F32),
            pltpu.VMEM((n, W_A), F32),
        ],
        compiler_params=pltpu.CompilerParams(
            dimension_semantics=("arbitrary",), vmem_limit_bytes=VMEM_LIMIT_BYTES),
        name="sample_layer",
    )(x_s, p_s, sa, sb_t, h0, *weights)


def kernel(x_prompt, x_sample, p_prompt, p_sample, state_conv_a, state_conv_b, state_h, g_norm, w_in, w_dw_a, b_dw_a, ln_g, ln_b, w_conv_b, b_conv_b, w_r, b_r, w_i, b_i, lam, w_out, w_pe, w_pg, g_final):
    depth = g_norm.shape[0]
    assert depth == 1, "single-layer step"
    nb, seq, _ = x_prompt.shape
    assert nb == SUBLANES and seq % TT == 0 and TT >= HIST_A
    row = lambda a: a.reshape(1, -1)
    win16, wout16, wpe16, wpg16 = _cast_weights(w_in[0], w_out[0], w_pe[0], w_pg[0])
    weights = (
        row(g_norm[0]), win16,
        w_dw_a[0], row(b_dw_a[0]), row(ln_g[0]), row(ln_b[0]),
        w_conv_b[0], row(b_conv_b[0]),
        jnp.concatenate([w_r[0], w_i[0]], axis=-1).astype(BF16), row(b_r[0]), row(b_i[0]), row(lam[0]),
        wout16, wpe16, wpg16, row(g_final),
    )

    y_prompt, na_t, nbuf_t, nh_p = _prompt_call(x_prompt, p_prompt[0], weights)
    untime = lambda a, k: jnp.swapaxes(a.reshape(k, nb, a.shape[-1]), 0, 1)
    na_p = untime(na_t, CONV_A_WIDTH - 1)[None]
    nb_p = untime(nbuf_t, CONV_B_WIDTH - 1)[None]

    n_s = x_sample.shape[0]
    assert x_sample.shape[1] == 1 and n_s % SAMPLE_BT == 0
    y_s, na_s, nbuf_s, nh_s = _sample_call(
        x_sample.reshape(n_s, D_MODEL), p_sample[0].reshape(n_s, PLE_DIM), jnp.swapaxes(state_conv_a[0], 0, 1),
        jnp.swapaxes(state_conv_b[0], 0, 1), state_h[0], weights)
    return (y_prompt, y_s.reshape(n_s, 1, D_MODEL), na_p, nb_p, nh_p[None],
            jnp.swapaxes(na_s, 0, 1)[None], jnp.swapaxes(nbuf_s, 0, 1)[None], nh_s[None])
```

```python
import functools

import jax
import jax.numpy as jnp
from jax import lax
from jax.experimental import pallas as pl
from jax.experimental.pallas import tpu as pltpu

D_MODEL = 1024
W_A = D_MODEL
W_B = D_MODEL
N_B_HEADS = 8
HEAD_B = W_B // N_B_HEADS
W_IN = 3 * W_A + 2 * W_B
CONV_A_WIDTH = 31
CONV_B_WIDTH = 4
RG_C = 8.0
PLE_DIM = 256
EPS = 1e-6

SUBLANES = 8
LANES = 128
MXU_N = 256
VMEM_LIMIT_BYTES = 60000 * 1024

TT = 64
HIST_A = 32
HIST_B = 4
SLAB_PITCH = TT + 8
ROW_TILE = 16
CONV_T = 8
CONV_PAIRS = (CONV_A_WIDTH - 1) // 2
SAMPLE_BT = 32
CAST_STEPS = 8

NC_D = D_MODEL // MXU_N
NC_GLU = 2 * W_A // MXU_N
NC_REST = (W_IN - 2 * W_A) // MXU_N
CH_GATE_A = 0
CH_BX = W_A // MXU_N
CH_GATE_B = (W_A + W_B) // MXU_N

F32 = jnp.float32
BF16 = jnp.bfloat16


def _sigmoid(x):
    return 0.5 * jnp.tanh(0.5 * x) + 0.5


def _silu(x):
    hx = 0.5 * x
    return hx * jnp.tanh(hx) + hx


def _dot(a, b):
    return jnp.dot(a, b, preferred_element_type=F32)


def _rmsnorm(x, g):
    ms = jnp.mean(x * x, axis=-1, keepdims=True)
    return x * lax.rsqrt(ms + EPS) * g


def _layernorm(x, g, b):
    mu = jnp.mean(x, axis=-1, keepdims=True)
    xc = x - mu
    var = jnp.mean(xc * xc, axis=-1, keepdims=True)
    return xc * lax.rsqrt(var + EPS) * g + b


def _rglru_coeffs(r, ig, xb, c_lam):
    log_a = c_lam * r
    a = jnp.exp(log_a)
    th = jnp.tanh(log_a)
    y = -2.0 * th / (1.0 - th)
    root = jnp.where(y == 0.0, 0.0, y * lax.rsqrt(y))
    bx = root * (ig * xb)
    return a, bx


def _aligned(v, m):
    return v if isinstance(v, int) else pl.multiple_of(v, m)


def _chunk_cols(c):
    return slice(c * MXU_N, (c + 1) * MXU_N)


def _lanes_of_chunks(ref, first, rows_sl):
    return jnp.concatenate([ref[first + c, rows_sl, :] for c in range(NC_D)], axis=-1)


def _prompt_kernel(nb, x_hbm, p_hbm, gn_ref, win_ref, wdw_ref, bdw_ref, lng_ref, lnb_ref,
                   wcb_ref, bcb_ref, wri_ref, br_ref, bi_ref, lam_ref, wout_ref, wpe_ref,
                   wpg_ref, gf_ref,
                   y_hbm, na_ref, nbuf_ref, nh_ref,
                   u_s, z_s, zb_s, vp_s, cv_s, a_s, mixa_s, mixb_s, xb16_s, h_s, wb_s,
                   xs_s, ps_s, ys_s, sem_in, sem_out):
    rows = TT * nb
    ha = HIST_A * nb
    hb = HIST_B * nb
    step = pl.program_id(0)
    n_steps = pl.num_programs(0)
    n_tiles = rows // ROW_TILE
    cur = slice(hb, hb + rows)
    heads = [slice(h * HEAD_B, (h + 1) * HEAD_B) for h in range(N_B_HEADS)]
    slot = step % 2

    def in_copies(s, sl):
        t0 = pl.multiple_of(s * TT, TT)
        cps = []
        for b in range(nb):
            dst_rows = pl.ds(b * SLAB_PITCH, TT)
            for j in range(D_MODEL // LANES):
                cps.append(pltpu.make_async_copy(
                    x_hbm.at[b, pl.ds(t0, TT), pl.ds(j * LANES, LANES)],
                    xs_s.at[sl, j, dst_rows, :], sem_in.at[sl]))
            for j in range(PLE_DIM // LANES):
                cps.append(pltpu.make_async_copy(
                    p_hbm.at[b, pl.ds(t0, TT), pl.ds(j * LANES, LANES)],
                    ps_s.at[sl, j, dst_rows, :], sem_in.at[sl]))
        return cps

    def out_copies(s):
        t0 = pl.multiple_of(s * TT, TT)
        return [pltpu.make_async_copy(
            ys_s.at[j, pl.ds(b * SLAB_PITCH, TT), :],
            y_hbm.at[b, pl.ds(t0, TT), pl.ds(j * LANES, LANES)], sem_out.at[0])
            for b in range(nb) for j in range(D_MODEL // LANES)]

    @pl.when(step == 0)
    def _first_fetch():
        for cp in in_copies(0, 0):
            cp.start()

    @pl.when(step + 1 < n_steps)
    def _prefetch():
        for cp in in_copies(step + 1, 1 - slot):
            cp.start()

    @pl.when(step == 0)
    def _init():
        vp_s[0:ha, :] = jnp.zeros((ha, W_A), F32)
        for c in range(NC_D):
            zb_s[CH_BX + c, rows:rows + hb, :] = jnp.zeros((hb, MXU_N), F32)
        h_s[...] = jnp.zeros_like(h_s)
        for k in range(CONV_A_WIDTH):
            wb_s[k * nb:(k + 1) * nb, :] = jnp.broadcast_to(wdw_ref[k:k + 1, :], (nb, W_A))
        for i in range(CONV_PAIRS):
            k = CONV_A_WIDTH + i
            wb_s[k * nb:(k + 1) * nb, :] = jnp.broadcast_to(
                wdw_ref[2 * i:2 * i + 1, :] + wdw_ref[2 * i + 1:2 * i + 2, :], (nb, W_A))

    @pl.when(step > 0)
    def _carry():
        vp_s[0:ha, :] = vp_s[rows:rows + ha, :]

    def tiled_loop(n_iter, tile_body, host_body=None):
        per_iter = n_tiles // n_iter

        def body(it, c):
            if host_body is not None:
                host_body(it)
            for q in range(per_iter):
                tile_body(_aligned((it * per_iter + q) * ROW_TILE, ROW_TILE))
            return c
        if host_body is not None:
            for it in range(n_iter):
                body(it, 0)
        else:
            lax.fori_loop(0, n_iter, body, 0)

    for cp in in_copies(step, slot):
        cp.wait()

    def time_rows(slab, groups, r0):
        t = lax.shift_right_logical(r0, SUBLANES.bit_length() - 1)
        return jnp.concatenate(
            [jnp.concatenate([slab[slot, j, pl.ds(t + q, nb, stride=SLAB_PITCH), :] for j in range(groups)],
                             axis=-1) for q in range(ROW_TILE // nb)], axis=0)

    def norm_in(r0):
        rs = pl.ds(r0, ROW_TILE)
        x = time_rows(xs_s, D_MODEL // LANES, r0)
        u_s[rs, :] = _rmsnorm(x, gn_ref[...]).astype(BF16)
    tiled_loop(2, norm_in)

    for c in range(NC_D):
        z_s[c] = _sigmoid(_dot(u_s[...], win_ref[:, _chunk_cols(NC_D + c)]))
        vp_s[ha:ha + rows, _chunk_cols(c)] = _dot(u_s[...], win_ref[:, _chunk_cols(c)]) * z_s[c]

    def carry_bx(c, carry):
        zb_s[CH_BX + c, 0:hb, :] = zb_s[CH_BX + c, rows:rows + hb, :]
        return carry
    lax.fori_loop(0, NC_D, carry_bx, 0)

    half = CONV_T // 2
    p_handoff = {}

    def conv_piece(blk, j):
        base = blk * (CONV_T * nb)
        ls = slice(j * LANES, (j + 1) * LANES)
        xp_cache, tap_cache = {}, {}

        def xp(d):
            if d not in xp_cache:
                off = (d + HIST_A - (CONV_A_WIDTH - 1)) * nb
                xp_cache[d] = vp_s[base + off:base + off + nb, ls]
            return xp_cache[d]

        def tap(idx):
            if idx not in tap_cache:
                tap_cache[idx] = wb_s[idx * nb:(idx + 1) * nb, ls]
            return tap_cache[idx]

        def add(acc, term):
            return term if acc is None else acc + term

        p = [None] * (half + 1)
        q = [None] * half
        u = [None] * half
        if blk > 0:
            p[0] = p_handoff[j]
        for n in range(half + CONV_PAIRS):
            xe = xp(2 * n)
            last = n == half + CONV_PAIRS - 1
            xo = None if last else xp(2 * n + 1)
            xs = None if last else xo + xp(2 * n + 2)
            for m in range(half + 1):
                i = n - m
                if not 0 <= i < CONV_PAIRS:
                    continue
                if m > 0 or blk == 0:
                    p[m] = add(p[m], tap(2 * i) * xe)
                if m < half:
                    q[m] = add(q[m], tap(2 * i + 1) * xo)
                    u[m] = add(u[m], tap(CONV_A_WIDTH + i) * xs)
        p_handoff[j] = p[half]
        bias = jnp.broadcast_to(bdw_ref[:, ls], (nb, LANES))
        w_last = tap(CONV_A_WIDTH - 1)
        for m in range(half):
            even = (p[m] + q[m]) + (bias + w_last * xp(2 * m + CONV_A_WIDTH - 1))
            odd = (u[m] - p[m + 1] - q[m]) + (bias + w_last * xp(2 * m + CONV_A_WIDTH))
            cv_s[base + 2 * m * nb:base + (2 * m + 1) * nb, ls] = even
            cv_s[base + (2 * m + 1) * nb:base + (2 * m + 2) * nb, ls] = odd

    def in_dot(ch):
        z = _dot(u_s[...], win_ref[:, _chunk_cols(NC_GLU + ch)])
        is_gate = not CH_BX <= ch < CH_GATE_B
        zb_s[ch, cur, :] = _silu(z) if is_gate else z

    pieces = [(blk, j) for j in range(W_A // LANES) for blk in range(TT // CONV_T)]
    next_dot = 0
    for n, piece in enumerate(pieces):
        want = -(-(n + 1) * NC_REST // len(pieces))
        while next_dot < want:
            in_dot(next_dot)
            next_dot += 1
        conv_piece(*piece)

    def conv_b(r0):
        rs = pl.ds(r0, ROW_TILE)
        for h in range(N_B_HEADS):
            ls = slice(h * HEAD_B, (h + 1) * HEAD_B)
            ch = CH_BX + (h * HEAD_B) // MXU_N
            lo = (h * HEAD_B) % MXU_N
            acc = jnp.broadcast_to(bcb_ref[:, ls], (ROW_TILE, HEAD_B))
            for k in range(CONV_B_WIDTH):
                off = (k + HIST_B - (CONV_B_WIDTH - 1)) * nb
                acc = acc + wcb_ref[k:k + 1, ls] * zb_s[ch, pl.ds(r0 + off, ROW_TILE), lo:lo + HEAD_B]
            a_s[rs, ls] = acc
            xb16_s[h, rs, :] = acc.astype(BF16)
    tiled_loop(8, conv_b)

    def gate_dot(it):
        for q in range(2):
            h = it * 2 + q
            bias = jnp.concatenate([br_ref[:, heads[h]], bi_ref[:, heads[h]]], axis=-1)
            z_s[h] = _sigmoid(_dot(xb16_s[h], wri_ref[h]) + bias)

    def mix_a(r0):
        rs = pl.ds(r0, ROW_TILE)
        v = _layernorm(cv_s[rs, :], lng_ref[...], lnb_ref[...])
        gate = _lanes_of_chunks(zb_s, CH_GATE_A, pl.ds(hb + r0, ROW_TILE))
        mixa_s[rs, :] = (_silu(v) * gate).astype(BF16)
    tiled_loop(N_B_HEADS // 2, mix_a, gate_dot)

    c_lam = -RG_C * jax.nn.softplus(-lam_ref[...])

    def out_a_dot(c):
        zb_s[CH_GATE_A + c, cur, :] = _dot(mixa_s[...], wout_ref[0:W_A, _chunk_cols(c)])

    h_run = [h_s[:, ls] for ls in heads]

    def coeffs(r0):
        rs = pl.ds(r0, ROW_TILE)
        for h, ls in enumerate(heads):
            a, bx = _rglru_coeffs(z_s[h, rs, 0:HEAD_B], z_s[h, rs, HEAD_B:2 * HEAD_B], a_s[rs, ls], c_lam[:, ls])
            hs = []
            for q in range(ROW_TILE // nb):
                tq = slice(q * nb, (q + 1) * nb)
                h_run[h] = a[tq] * h_run[h] + bx[tq]
                hs.append(h_run[h])
            ch, lo = CH_GATE_B + (h * HEAD_B) // MXU_N, (h * HEAD_B) % MXU_N
            gate = zb_s[ch, hb + r0:hb + r0 + ROW_TILE, lo:lo + HEAD_B]
            mixb_s[rs, ls] = (jnp.concatenate(hs, axis=0) * gate).astype(BF16)
    tiled_loop(NC_D, coeffs, out_a_dot)
    for h, ls in enumerate(heads):
        h_s[:, ls] = h_run[h]

    def slab_cols(slab, c):
        lanes = []
        for j in range(c * (MXU_N // LANES), (c + 1) * (MXU_N // LANES)):
            lanes.append(jnp.concatenate(
                [slab[slot, j, pl.ds(t, nb, stride=SLAB_PITCH), :] for t in range(TT)], axis=0))
        return jnp.concatenate(lanes, axis=-1)

    for c in range(NC_D):
        h1 = (_dot(mixb_s[...], wout_ref[W_A:W_A + W_B, _chunk_cols(c)])
              + zb_s[CH_GATE_A + c, cur, :] + slab_cols(xs_s, c))
        cv_s[:, _chunk_cols(c)] = h1
        u_s[:, _chunk_cols(c)] = h1.astype(BF16)

    p16 = slab_cols(ps_s, 0).astype(BF16)
    for c in range(NC_D):
        z_s[c] = _sigmoid(_dot(u_s[...], wpg_ref[:, _chunk_cols(c)]))
        cv_s[:, _chunk_cols(c)] = cv_s[:, _chunk_cols(c)] + _dot(p16, wpe_ref[:, _chunk_cols(c)]) * z_s[c]

    @pl.when(step > 0)
    def _drain_prev():
        for cp in out_copies(step - 1):
            cp.wait()

    def finish(r0):
        rs = pl.ds(r0, ROW_TILE)
        y = _rmsnorm(cv_s[rs, :], gf_ref[...])
        t = lax.shift_right_logical(r0, SUBLANES.bit_length() - 1)
        for q in range(ROW_TILE // nb):
            for j in range(D_MODEL // LANES):
                ys_s[j, pl.ds(t + q, nb, stride=SLAB_PITCH), :] = y[q * nb:(q + 1) * nb, j * LANES:(j + 1) * LANES]
    tiled_loop(2, finish)

    for cp in out_copies(step):
        cp.start()

    @pl.when(step == n_steps - 1)
    def _drain_last():
        for cp in out_copies(step):
            cp.wait()

    @pl.when(step == n_steps - 1)
    def _state_out():
        na = (CONV_A_WIDTH - 1) * nb
        nbb = (CONV_B_WIDTH - 1) * nb
        na_ref[...] = vp_s[ha + rows - na:ha + rows, :]
        for c in range(NC_D):
            nbuf_ref[:, c * MXU_N:(c + 1) * MXU_N] = zb_s[CH_BX + c, hb + rows - nbb:hb + rows, :]
        nh_ref[...] = h_s[...]


def _sample_kernel(x_ref, p_ref, sa_ref, sb_ref, h0_ref, gn_ref, win_ref, wdw_ref, bdw_ref,
                   lng_ref, lnb_ref, wcb_ref, bcb_ref, wri_ref, br_ref, bi_ref, lam_ref,
                   wout_ref, wpe_ref, wpg_ref, gf_ref,
                   y_ref, na_ref, nbuf_ref, nh_ref,
                   u_s, z_s, v_s, cvp_s):
    step = pl.program_id(0)
    ka = CONV_A_WIDTH - 1
    kb = CONV_B_WIDTH - 1

    @pl.when(step == 0)
    def _project():
        u = _rmsnorm(x_ref[...], gn_ref[...]).astype(BF16)
        u_s[...] = u
        za = _dot(u, win_ref[:, 0:2 * W_A])
        v_s[...] = za[:, 0:W_A] * _sigmoid(za[:, W_A:2 * W_A])
        z_s[...] = _dot(u, win_ref[:, 2 * W_A:W_IN])

    seqs = pl.ds(pl.multiple_of(step * SAMPLE_BT, SAMPLE_BT), SAMPLE_BT)
    acc = wdw_ref[0:1, :] * sa_ref[0]
    for k in range(1, ka):
        acc = acc + wdw_ref[k:k + 1, :] * sa_ref[k]
        na_ref[k - 1] = sa_ref[k]
    cvp_s[seqs, :] = acc
    na_ref[ka - 1] = v_s[seqs, :]

    @pl.when(step == pl.num_programs(0) - 1)
    def _rest():
        v = v_s[...]
        cv = cvp_s[...] + wdw_ref[ka:ka + 1, :] * v + bdw_ref[...]
        v = _silu(_layernorm(cv, lng_ref[...], lnb_ref[...])) * _silu(z_s[:, 0:W_A])
        mix_a = v.astype(BF16)

        b_x = z_s[:, W_A:W_A + W_B]
        xb = wcb_ref[kb:kb + 1, :] * b_x + bcb_ref[...]
        for k in range(kb):
            xb = xb + wcb_ref[k:k + 1, :] * sb_ref[k]
        for k in range(kb - 1):
            nbuf_ref[k] = sb_ref[k + 1]
        nbuf_ref[kb - 1] = b_x

        xb16 = xb.astype(BF16)
        c_lam = -RG_C * jax.nn.softplus(-lam_ref[...])
        a_parts, b_parts = [], []
        for h in range(N_B_HEADS):
            ls = slice(h * HEAD_B, (h + 1) * HEAD_B)
            g = _dot(xb16[:, ls], wri_ref[h])
            a, bx = _rglru_coeffs(_sigmoid(g[:, 0:HEAD_B] + br_ref[:, ls]),
                                  _sigmoid(g[:, HEAD_B:2 * HEAD_B] + bi_ref[:, ls]), xb[:, ls], c_lam[:, ls])
            a_parts.append(a)
            b_parts.append(bx)
        hn = jnp.concatenate(a_parts, axis=-1) * h0_ref[...] + jnp.concatenate(b_parts, axis=-1)
        nh_ref[...] = hn
        mix_b = (hn * _silu(z_s[:, W_A + W_B:W_A + 2 * W_B])).astype(BF16)

        mix = _dot(mix_a, wout_ref[0:W_A, :]) + _dot(mix_b, wout_ref[W_A:W_A + W_B, :])
        h1 = x_ref[...] + mix
        pe = _dot(p_ref[...].astype(BF16), wpe_ref[...]) * _sigmoid(_dot(h1.astype(BF16), wpg_ref[...]))
        y_ref[...] = _rmsnorm(h1 + pe, gf_ref[...])


def _cast_kernel(*refs):
    n = len(refs) // 2
    for src, dst in zip(refs[:n], refs[n:]):
        dst[...] = src[...].astype(BF16)


def _cast_weights(*ws):
    spec = lambda w: pl.BlockSpec((w.shape[0] // CAST_STEPS, w.shape[1]), lambda i: (i, 0))
    return pl.pallas_call(
        _cast_kernel,
        grid=(CAST_STEPS,),
        in_specs=[spec(w) for w in ws],
        out_specs=[spec(w) for w in ws],
        out_shape=[jax.ShapeDtypeStruct(w.shape, BF16) for w in ws],
        compiler_params=pltpu.CompilerParams(dimension_semantics=("arbitrary",)),
        name="cast_weights",
    )(*ws)


def _const_spec(shape):
    zeros = (0,) * len(shape)
    return pl.BlockSpec(shape, lambda i: zeros, pipeline_mode=pl.Buffered(1))


def _weight_specs():
    return [
        _const_spec((1, D_MODEL)),
        _const_spec((D_MODEL, W_IN)),
        _const_spec((CONV_A_WIDTH, W_A)),
        _const_spec((1, W_A)),
        _const_spec((1, W_A)),
        _const_spec((1, W_A)),
        _const_spec((CONV_B_WIDTH, W_B)),
        _const_spec((1, W_B)),
        _const_spec((N_B_HEADS, HEAD_B, 2 * HEAD_B)),
        _const_spec((1, W_B)),
        _const_spec((1, W_B)),
        _const_spec((1, W_B)),
        _const_spec((W_A + W_B, D_MODEL)),
        _const_spec((PLE_DIM, D_MODEL)),
        _const_spec((D_MODEL, D_MODEL)),
        _const_spec((1, D_MODEL)),
    ]


def _prompt_call(x, p, weights):
    nb, seq, _ = x.shape
    rows = TT * nb
    n_a = (CONV_A_WIDTH - 1) * nb
    n_b = (CONV_B_WIDTH - 1) * nb
    slab_rows = nb * SLAB_PITCH
    hbm = pl.BlockSpec(memory_space=pl.ANY)
    out_shape = (
        jax.ShapeDtypeStruct((nb, seq, D_MODEL), F32),
        jax.ShapeDtypeStruct((n_a, W_A), F32),
        jax.ShapeDtypeStruct((n_b, W_B), F32),
        jax.ShapeDtypeStruct((nb, W_B), F32),
    )
    const_out = lambda shape: pl.BlockSpec(shape, lambda i: (0, 0))
    return pl.pallas_call(
        functools.partial(_prompt_kernel, nb),
        grid=(seq // TT,),
        in_specs=[hbm, hbm] + _weight_specs(),
        out_specs=(hbm, const_out((n_a, W_A)), const_out((n_b, W_B)), const_out((nb, W_B))),
        out_shape=out_shape,
        scratch_shapes=[
            pltpu.VMEM((rows, D_MODEL), BF16),
            pltpu.VMEM((NC_GLU, rows, MXU_N), F32),
            pltpu.VMEM((NC_REST, HIST_B * nb + rows, MXU_N), F32),
            pltpu.VMEM(((HIST_A + TT) * nb, W_A), F32),
            pltpu.VMEM((rows, D_MODEL), F32),
            pltpu.VMEM((rows, W_B), F32),
            pltpu.VMEM((rows, W_A), BF16),
            pltpu.VMEM((rows, W_B), BF16),
            pltpu.VMEM((N_B_HEADS, rows, HEAD_B), BF16),
            pltpu.VMEM((nb, W_B), F32),
            pltpu.VMEM(((CONV_A_WIDTH + CONV_PAIRS) * nb, W_A), F32),
            pltpu.VMEM((2, D_MODEL // LANES, slab_rows, LANES), F32),
            pltpu.VMEM((2, PLE_DIM // LANES, slab_rows, LANES), F32),
            pltpu.VMEM((D_MODEL // LANES, slab_rows, LANES), F32),
            pltpu.SemaphoreType.DMA((2,)),
            pltpu.SemaphoreType.DMA((1,)),
        ],
        compiler_params=pltpu.CompilerParams(
            dimension_semantics=("arbitrary",), vmem_limit_bytes=VMEM_LIMIT_BYTES),
        name="prompt_layer",
    )(x, p, *weights)


def _sample_call(x_s, p_s, sa, sb_t, h0, weights):
    n = x_s.shape[0]
    ka = CONV_A_WIDTH - 1
    kb = CONV_B_WIDTH - 1
    out_shape = (
        jax.ShapeDtypeStruct((n, D_MODEL), F32),
        jax.ShapeDtypeStruct((ka, n, W_A), F32),
        jax.ShapeDtypeStruct((kb, n, W_B), F32),
        jax.ShapeDtypeStruct((n, W_B), F32),
    )
    full2 = lambda shape: pl.BlockSpec(shape, lambda i: (0, 0))
    full3 = lambda shape: pl.BlockSpec(shape, lambda i: (0, 0, 0))
    return pl.pallas_call(
        _sample_kernel,
        grid=(n // SAMPLE_BT,),
        in_specs=[full2((n, D_MODEL)), full2((n, PLE_DIM)),
                  pl.BlockSpec((ka, SAMPLE_BT, W_A), lambda i: (0, i, 0)),
                  full3((kb, n, W_B)), full2((n, W_B))] + _weight_specs(),
        out_specs=(full2((n, D_MODEL)),
                   pl.BlockSpec((ka, SAMPLE_BT, W_A), lambda i: (0, i, 0)),
                   full3((kb, n, W_B)), full2((n, W_B))),
        out_shape=out_shape,
        scratch_shapes=[
            pltpu.VMEM((n, D_MODEL), BF16),
            pltpu.VMEM((n, W_IN - 2 * W_A), F32),
            pltpu.VMEM((n, W_A), F32),
            pltpu.VMEM((n, W_A), F32),
        ],
        compiler_params=pltpu.CompilerParams(
            dimension_semantics=("arbitrary",), vmem_limit_bytes=VMEM_LIMIT_BYTES),
        name="sample_layer",
    )(x_s, p_s, sa, sb_t, h0, *weights)


def kernel(x_prompt, x_sample, p_prompt, p_sample, state_conv_a, state_conv_b, state_h, g_norm, w_in, w_dw_a, b_dw_a, ln_g, ln_b, w_conv_b, b_conv_b, w_r, b_r, w_i, b_i, lam, w_out, w_pe, w_pg, g_final):
    depth = g_norm.shape[0]
    assert depth == 1, "single-layer step"
    nb, seq, _ = x_prompt.shape
    assert nb == SUBLANES and seq % TT == 0 and TT >= HIST_A
    row = lambda a: a.reshape(1, -1)
    win16, wout16, wpe16, wpg16 = _cast_weights(w_in[0], w_out[0], w_pe[0], w_pg[0])
    weights = (
        row(g_norm[0]), win16,
        w_dw_a[0], row(b_dw_a[0]), row(ln_g[0]), row(ln_b[0]),
        w_conv_b[0], row(b_conv_b[0]),
        jnp.concatenate([w_r[0], w_i[0]], axis=-1).astype(BF16), row(b_r[0]), row(b_i[0]), row(lam[0]),
        wout16, wpe16, wpg16, row(g_final),
    )

    y_prompt, na_t, nbuf_t, nh_p = _prompt_call(x_prompt, p_prompt[0], weights)
    untime = lambda a, k: jnp.swapaxes(a.reshape(k, nb, a.shape[-1]), 0, 1)
    na_p = untime(na_t, CONV_A_WIDTH - 1)[None]
    nb_p = untime(nbuf_t, CONV_B_WIDTH - 1)[None]

    n_s = x_sample.shape[0]
    assert x_sample.shape[1] == 1 and n_s % SAMPLE_BT == 0
    y_s, na_s, nbuf_s, nh_s = _sample_call(
        x_sample.reshape(n_s, D_MODEL), p_sample[0].reshape(n_s, PLE_DIM), jnp.swapaxes(state_conv_a[0], 0, 1),
        jnp.swapaxes(state_conv_b[0], 0, 1), state_h[0], weights)
    return (y_prompt, y_s.reshape(n_s, 1, D_MODEL), na_p, nb_p, nh_p[None],
            jnp.swapaxes(na_s, 0, 1)[None], jnp.swapaxes(nbuf_s, 0, 1)[None], nh_s[None])
```

```python
import functools

import jax
import jax.numpy as jnp
from jax import lax
from jax.experimental import pallas as pl
from jax.experimental.pallas import tpu as pltpu

D_MODEL = 1024
W_A = D_MODEL
W_B = D_MODEL
N_B_HEADS = 8
HEAD_B = W_B // N_B_HEADS
W_IN = 3 * W_A + 2 * W_B
CONV_A_WIDTH = 31
CONV_B_WIDTH = 4
RG_C = 8.0
PLE_DIM = 256
EPS = 1e-6

SUBLANES = 8
LANES = 128
MXU_N = 256
VMEM_LIMIT_BYTES = 60000 * 1024

TT = 64
HIST_A = 32
HIST_B = 4
SLAB_PITCH = TT + 8
ROW_TILE = 16
CONV_T = 8
CONV_PAIRS = (CONV_A_WIDTH - 1) // 2
SAMPLE_BT = 32
CAST_STEPS = 8

NC_D = D_MODEL // MXU_N
NC_GLU = 2 * W_A // MXU_N
NC_REST = (W_IN - 2 * W_A) // MXU_N
CH_GATE_A = 0
CH_BX = W_A // MXU_N
CH_GATE_B = (W_A + W_B) // MXU_N

F32 = jnp.float32
BF16 = jnp.bfloat16


def _sigmoid(x):
    return 0.5 * jnp.tanh(0.5 * x) + 0.5


def _silu(x):
    hx = 0.5 * x
    return hx * jnp.tanh(hx) + hx


def _dot(a, b):
    return jnp.dot(a, b, preferred_element_type=F32)


def _rmsnorm(x, g):
    ms = jnp.mean(x * x, axis=-1, keepdims=True)
    return x * lax.rsqrt(ms + EPS) * g


def _layernorm(x, g, b):
    mu = jnp.mean(x, axis=-1, keepdims=True)
    xc = x - mu
    var = jnp.mean(xc * xc, axis=-1, keepdims=True)
    return xc * lax.rsqrt(var + EPS) * g + b


def _rglru_coeffs(r, ig, xb, c_lam):
    log_a = c_lam * r
    a = jnp.exp(log_a)
    th = jnp.tanh(log_a)
    y = -2.0 * th / (1.0 - th)
    root = jnp.where(y == 0.0, 0.0, y * lax.rsqrt(y))
    bx = root * (ig * xb)
    return a, bx


def _aligned(v, m):
    return v if isinstance(v, int) else pl.multiple_of(v, m)


def _chunk_cols(c):
    return slice(c * MXU_N, (c + 1) * MXU_N)


def _lanes_of_chunks(ref, first, rows_sl):
    return jnp.concatenate([ref[first + c, rows_sl, :] for c in range(NC_D)], axis=-1)


def _prompt_kernel(nb, x_hbm, p_hbm, gn_ref, win_ref, wdw_ref, bdw_ref, lng_ref, lnb_ref,
                   wcb_ref, bcb_ref, wri_ref, br_ref, bi_ref, lam_ref, wout_ref, wpe_ref,
                   wpg_ref, gf_ref,
                   y_hbm, na_ref, nbuf_ref, nh_ref,
                   u_s, z_s, zb_s, vp_s, cv_s, a_s, mixa_s, mixb_s, xb16_s, h_s, wb_s,
                   xs_s, ps_s, ys_s, sem_in, sem_out):
    rows = TT * nb
    ha = HIST_A * nb
    hb = HIST_B * nb
    step = pl.program_id(0)
    n_steps = pl.num_programs(0)
    n_tiles = rows // ROW_TILE
    cur = slice(hb, hb + rows)
    heads = [slice(h * HEAD_B, (h + 1) * HEAD_B) for h in range(N_B_HEADS)]
    slot = step % 2

    def in_copies(s, sl):
        t0 = pl.multiple_of(s * TT, TT)
        cps = []
        for b in range(nb):
            dst_rows = pl.ds(b * SLAB_PITCH, TT)
            for j in range(D_MODEL // LANES):
                cps.append(pltpu.make_async_copy(
                    x_hbm.at[b, pl.ds(t0, TT), pl.ds(j * LANES, LANES)],
                    xs_s.at[sl, j, dst_rows, :], sem_in.at[sl]))
            for j in range(PLE_DIM // LANES):
                cps.append(pltpu.make_async_copy(
                    p_hbm.at[b, pl.ds(t0, TT), pl.ds(j * LANES, LANES)],
                    ps_s.at[sl, j, dst_rows, :], sem_in.at[sl]))
        return cps

    def out_copies(s):
        t0 = pl.multiple_of(s * TT, TT)
        return [pltpu.make_async_copy(
            ys_s.at[j, pl.ds(b * SLAB_PITCH, TT), :],
            y_hbm.at[b, pl.ds(t0, TT), pl.ds(j * LANES, LANES)], sem_out.at[0])
            for b in range(nb) for j in range(D_MODEL // LANES)]

    @pl.when(step == 0)
    def _first_fetch():
        for cp in in_copies(0, 0):
            cp.start()

    @pl.when(step + 1 < n_steps)
    def _prefetch():
        for cp in in_copies(step + 1, 1 - slot):
            cp.start()

    @pl.when(step == 0)
    def _init():
        vp_s[0:ha, :] = jnp.zeros((ha, W_A), F32)
        for c in range(NC_D):
            zb_s[CH_BX + c, rows:rows + hb, :] = jnp.zeros((hb, MXU_N), F32)
        h_s[...] = jnp.zeros_like(h_s)
        for k in range(CONV_A_WIDTH):
            wb_s[k * nb:(k + 1) * nb, :] = jnp.broadcast_to(wdw_ref[k:k + 1, :], (nb, W_A))
        for i in range(CONV_PAIRS):
            k = CONV_A_WIDTH + i
            wb_s[k * nb:(k + 1) * nb, :] = jnp.broadcast_to(
                wdw_ref[2 * i:2 * i + 1, :] + wdw_ref[2 * i + 1:2 * i + 2, :], (nb, W_A))

    @pl.when(step > 0)
    def _carry():
        vp_s[0:ha, :] = vp_s[rows:rows + ha, :]

    def tiled_loop(n_iter, tile_body, host_body=None):
        per_iter = n_tiles // n_iter

        def body(it, c):
            if host_body is not None:
                host_body(it)
            for q in range(per_iter):
                tile_body(_aligned((it * per_iter + q) * ROW_TILE, ROW_TILE))
            return c
        if host_body is not None:
            for it in range(n_iter):
                body(it, 0)
        else:
            lax.fori_loop(0, n_iter, body, 0)

    for cp in in_copies(step, slot):
        cp.wait()

    def time_rows(slab, groups, r0):
        t = lax.shift_right_logical(r0, SUBLANES.bit_length() - 1)
        return jnp.concatenate(
            [jnp.concatenate([slab[slot, j, pl.ds(t + q, nb, stride=SLAB_PITCH), :] for j in range(groups)],
                             axis=-1) for q in range(ROW_TILE // nb)], axis=0)

    def norm_in(r0):
        rs = pl.ds(r0, ROW_TILE)
        x = time_rows(xs_s, D_MODEL // LANES, r0)
        u_s[rs, :] = _rmsnorm(x, gn_ref[...]).astype(BF16)
    tiled_loop(1, norm_in)

    for c in range(NC_D):
        z_s[c] = _sigmoid(_dot(u_s[...], win_ref[:, _chunk_cols(NC_D + c)]))
        vp_s[ha:ha + rows, _chunk_cols(c)] = _dot(u_s[...], win_ref[:, _chunk_cols(c)]) * z_s[c]

    def carry_bx(c, carry):
        zb_s[CH_BX + c, 0:hb, :] = zb_s[CH_BX + c, rows:rows + hb, :]
        return carry
    lax.fori_loop(0, NC_D, carry_bx, 0)

    half = CONV_T // 2
    p_handoff = {}

    def conv_piece(blk, j):
        base = blk * (CONV_T * nb)
        ls = slice(j * LANES, (j + 1) * LANES)
        xp_cache, tap_cache = {}, {}

        def xp(d):
            if d not in xp_cache:
                off = (d + HIST_A - (CONV_A_WIDTH - 1)) * nb
                xp_cache[d] = vp_s[base + off:base + off + nb, ls]
            return xp_cache[d]

        def tap(idx):
            if idx not in tap_cache:
                tap_cache[idx] = wb_s[idx * nb:(idx + 1) * nb, ls]
            return tap_cache[idx]

        def add(acc, term):
            return term if acc is None else acc + term

        p = [None] * (half + 1)
        q = [None] * half
        u = [None] * half
        if blk > 0:
            p[0] = p_handoff[j]
        for n in range(half + CONV_PAIRS):
            xe = xp(2 * n)
            last = n == half + CONV_PAIRS - 1
            xo = None if last else xp(2 * n + 1)
            xs = None if last else xo + xp(2 * n + 2)
            for m in range(half + 1):
                i = n - m
                if not 0 <= i < CONV_PAIRS:
                    continue
                if m > 0 or blk == 0:
                    p[m] = add(p[m], tap(2 * i) * xe)
                if m < half:
                    q[m] = add(q[m], tap(2 * i + 1) * xo)
                    u[m] = add(u[m], tap(CONV_A_WIDTH + i) * xs)
        p_handoff[j] = p[half]
        bias = jnp.broadcast_to(bdw_ref[:, ls], (nb, LANES))
        w_last = tap(CONV_A_WIDTH - 1)
        for m in range(half):
            even = (p[m] + q[m]) + (bias + w_last * xp(2 * m + CONV_A_WIDTH - 1))
            odd = (u[m] - p[m + 1] - q[m]) + (bias + w_last * xp(2 * m + CONV_A_WIDTH))
            cv_s[base + 2 * m * nb:base + (2 * m + 1) * nb, ls] = even
            cv_s[base + (2 * m + 1) * nb:base + (2 * m + 2) * nb, ls] = odd

    def in_dot(ch):
        z = _dot(u_s[...], win_ref[:, _chunk_cols(NC_GLU + ch)])
        is_gate = not CH_BX <= ch < CH_GATE_B
        zb_s[ch, cur, :] = _silu(z) if is_gate else z

    pieces = [(blk, j) for j in range(W_A // LANES) for blk in range(TT // CONV_T)]
    next_dot = 0
    for n, piece in enumerate(pieces):
        want = -(-(n + 1) * NC_REST // len(pieces))
        while next_dot < want:
            in_dot(next_dot)
            next_dot += 1
        conv_piece(*piece)

    def conv_b(r0):
        rs = pl.ds(r0, ROW_TILE)
        for h in range(N_B_HEADS):
            ls = slice(h * HEAD_B, (h + 1) * HEAD_B)
            ch = CH_BX + (h * HEAD_B) // MXU_N
            lo = (h * HEAD_B) % MXU_N
            acc = jnp.broadcast_to(bcb_ref[:, ls], (ROW_TILE, HEAD_B))
            for k in range(CONV_B_WIDTH):
                off = (k + HIST_B - (CONV_B_WIDTH - 1)) * nb
                acc = acc + wcb_ref[k:k + 1, ls] * zb_s[ch, pl.ds(r0 + off, ROW_TILE), lo:lo + HEAD_B]
            a_s[rs, ls] = acc
            xb16_s[h, rs, :] = acc.astype(BF16)
    tiled_loop(4, conv_b)

    def gate_dot(it):
        for q in range(2):
            h = it * 2 + q
            bias = jnp.concatenate([br_ref[:, heads[h]], bi_ref[:, heads[h]]], axis=-1)
            z_s[h] = _sigmoid(_dot(xb16_s[h], wri_ref[h]) + bias)

    def mix_a(r0):
        rs = pl.ds(r0, ROW_TILE)
        v = _layernorm(cv_s[rs, :], lng_ref[...], lnb_ref[...])
        gate = _lanes_of_chunks(zb_s, CH_GATE_A, pl.ds(hb + r0, ROW_TILE))
        mixa_s[rs, :] = (_silu(v) * gate).astype(BF16)
    tiled_loop(N_B_HEADS // 2, mix_a, gate_dot)

    c_lam = -RG_C * jax.nn.softplus(-lam_ref[...])

    def out_a_dot(c):
        zb_s[CH_GATE_A + c, cur, :] = _dot(mixa_s[...], wout_ref[0:W_A, _chunk_cols(c)])

    h_run = [h_s[:, ls] for ls in heads]

    def coeffs(r0):
        rs = pl.ds(r0, ROW_TILE)
        for h, ls in enumerate(heads):
            a, bx = _rglru_coeffs(z_s[h, rs, 0:HEAD_B], z_s[h, rs, HEAD_B:2 * HEAD_B], a_s[rs, ls], c_lam[:, ls])
            hs = []
            for q in range(ROW_TILE // nb):
                tq = slice(q * nb, (q + 1) * nb)
                h_run[h] = a[tq] * h_run[h] + bx[tq]
                hs.append(h_run[h])
            ch, lo = CH_GATE_B + (h * HEAD_B) // MXU_N, (h * HEAD_B) % MXU_N
            gate = zb_s[ch, hb + r0:hb + r0 + ROW_TILE, lo:lo + HEAD_B]
            mixb_s[rs, ls] = (jnp.concatenate(hs, axis=0) * gate).astype(BF16)
    tiled_loop(NC_D, coeffs, out_a_dot)
    for h, ls in enumerate(heads):
        h_s[:, ls] = h_run[h]

    def slab_cols(slab, c):
        lanes = []
        for j in range(c * (MXU_N // LANES), (c + 1) * (MXU_N // LANES)):
            lanes.append(jnp.concatenate(
                [slab[slot, j, pl.ds(t, nb, stride=SLAB_PITCH), :] for t in range(TT)], axis=0))
        return jnp.concatenate(lanes, axis=-1)

    for c in range(NC_D):
        h1 = (_dot(mixb_s[...], wout_ref[W_A:W_A + W_B, _chunk_cols(c)])
              + zb_s[CH_GATE_A + c, cur, :] + slab_cols(xs_s, c))
        cv_s[:, _chunk_cols(c)] = h1
        u_s[:, _chunk_cols(c)] = h1.astype(BF16)

    p16 = slab_cols(ps_s, 0).astype(BF16)
    for c in range(NC_D):
        z_s[c] = _sigmoid(_dot(u_s[...], wpg_ref[:, _chunk_cols(c)]))
        cv_s[:, _chunk_cols(c)] = cv_s[:, _chunk_cols(c)] + _dot(p16, wpe_ref[:, _chunk_cols(c)]) * z_s[c]

    @pl.when(step > 0)
    def _drain_prev():
        for cp in out_copies(step - 1):
            cp.wait()

    def finish(r0):
        rs = pl.ds(r0, ROW_TILE)
        y = _rmsnorm(cv_s[rs, :], gf_ref[...])
        t = lax.shift_right_logical(r0, SUBLANES.bit_length() - 1)
        for q in range(ROW_TILE // nb):
            for j in range(D_MODEL // LANES):
                ys_s[j, pl.ds(t + q, nb, stride=SLAB_PITCH), :] = y[q * nb:(q + 1) * nb, j * LANES:(j + 1) * LANES]
    tiled_loop(1, finish)

    for cp in out_copies(step):
        cp.start()

    @pl.when(step == n_steps - 1)
    def _drain_last():
        for cp in out_copies(step):
            cp.wait()

    @pl.when(step == n_steps - 1)
    def _state_out():
        na = (CONV_A_WIDTH - 1) * nb
        nbb = (CONV_B_WIDTH - 1) * nb
        na_ref[...] = vp_s[ha + rows - na:ha + rows, :]
        for c in range(NC_D):
            nbuf_ref[:, c * MXU_N:(c + 1) * MXU_N] = zb_s[CH_BX + c, hb + rows - nbb:hb + rows, :]
        nh_ref[...] = h_s[...]


def _sample_kernel(x_ref, p_ref, sa_ref, sb_ref, h0_ref, gn_ref, win_ref, wdw_ref, bdw_ref,
                   lng_ref, lnb_ref, wcb_ref, bcb_ref, wri_ref, br_ref, bi_ref, lam_ref,
                   wout_ref, wpe_ref, wpg_ref, gf_ref,
                   y_ref, na_ref, nbuf_ref, nh_ref,
                   u_s, z_s, v_s, cvp_s):
    step = pl.program_id(0)
    ka = CONV_A_WIDTH - 1
    kb = CONV_B_WIDTH - 1

    @pl.when(step == 0)
    def _project():
        u = _rmsnorm(x_ref[...], gn_ref[...]).astype(BF16)
        u_s[...] = u
        za = _dot(u, win_ref[:, 0:2 * W_A])
        v_s[...] = za[:, 0:W_A] * _sigmoid(za[:, W_A:2 * W_A])
        z_s[...] = _dot(u, win_ref[:, 2 * W_A:W_IN])

    seqs = pl.ds(pl.multiple_of(step * SAMPLE_BT, SAMPLE_BT), SAMPLE_BT)
    acc = wdw_ref[0:1, :] * sa_ref[0]
    for k in range(1, ka):
        acc = acc + wdw_ref[k:k + 1, :] * sa_ref[k]
        na_ref[k - 1] = sa_ref[k]
    cvp_s[seqs, :] = acc
    na_ref[ka - 1] = v_s[seqs, :]

    @pl.when(step == pl.num_programs(0) - 1)
    def _rest():
        v = v_s[...]
        cv = cvp_s[...] + wdw_ref[ka:ka + 1, :] * v + bdw_ref[...]
        v = _silu(_layernorm(cv, lng_ref[...], lnb_ref[...])) * _silu(z_s[:, 0:W_A])
        mix_a = v.astype(BF16)

        b_x = z_s[:, W_A:W_A + W_B]
        xb = wcb_ref[kb:kb + 1, :] * b_x + bcb_ref[...]
        for k in range(kb):
            xb = xb + wcb_ref[k:k + 1, :] * sb_ref[k]
        for k in range(kb - 1):
            nbuf_ref[k] = sb_ref[k + 1]
        nbuf_ref[kb - 1] = b_x

        xb16 = xb.astype(BF16)
        c_lam = -RG_C * jax.nn.softplus(-lam_ref[...])
        a_parts, b_parts = [], []
        for h in range(N_B_HEADS):
            ls = slice(h * HEAD_B, (h + 1) * HEAD_B)
            g = _dot(xb16[:, ls], wri_ref[h])
            a, bx = _rglru_coeffs(_sigmoid(g[:, 0:HEAD_B] + br_ref[:, ls]),
                                  _sigmoid(g[:, HEAD_B:2 * HEAD_B] + bi_ref[:, ls]), xb[:, ls], c_lam[:, ls])
            a_parts.append(a)
            b_parts.append(bx)
        hn = jnp.concatenate(a_parts, axis=-1) * h0_ref[...] + jnp.concatenate(b_parts, axis=-1)
        nh_ref[...] = hn
        mix_b = (hn * _silu(z_s[:, W_A + W_B:W_A + 2 * W_B])).astype(BF16)

        mix = _dot(mix_a, wout_ref[0:W_A, :]) + _dot(mix_b, wout_ref[W_A:W_A + W_B, :])
        h1 = x_ref[...] + mix
        pe = _dot(p_ref[...].astype(BF16), wpe_ref[...]) * _sigmoid(_dot(h1.astype(BF16), wpg_ref[...]))
        y_ref[...] = _rmsnorm(h1 + pe, gf_ref[...])


def _cast_kernel(*refs):
    n = len(refs) // 2
    for src, dst in zip(refs[:n], refs[n:]):
        dst[...] = src[...].astype(BF16)


def _cast_weights(*ws):
    spec = lambda w: pl.BlockSpec((w.shape[0] // CAST_STEPS, w.shape[1]), lambda i: (i, 0))
    return pl.pallas_call(
        _cast_kernel,
        grid=(CAST_STEPS,),
        in_specs=[spec(w) for w in ws],
        out_specs=[spec(w) for w in ws],
        out_shape=[jax.ShapeDtypeStruct(w.shape, BF16) for w in ws],
        compiler_params=pltpu.CompilerParams(dimension_semantics=("arbitrary",)),
        name="cast_weights",
    )(*ws)


def _const_spec(shape):
    zeros = (0,) * len(shape)
    return pl.BlockSpec(shape, lambda i: zeros, pipeline_mode=pl.Buffered(1))


def _weight_specs():
    return [
        _const_spec((1, D_MODEL)),
        _const_spec((D_MODEL, W_IN)),
        _const_spec((CONV_A_WIDTH, W_A)),
        _const_spec((1, W_A)),
        _const_spec((1, W_A)),
        _const_spec((1, W_A)),
        _const_spec((CONV_B_WIDTH, W_B)),
        _const_spec((1, W_B)),
        _const_spec((N_B_HEADS, HEAD_B, 2 * HEAD_B)),
        _const_spec((1, W_B)),
        _const_spec((1, W_B)),
        _const_spec((1, W_B)),
        _const_spec((W_A + W_B, D_MODEL)),
        _const_spec((PLE_DIM, D_MODEL)),
        _const_spec((D_MODEL, D_MODEL)),
        _const_spec((1, D_MODEL)),
    ]


def _prompt_call(x, p, weights):
    nb, seq, _ = x.shape
    rows = TT * nb
    n_a = (CONV_A_WIDTH - 1) * nb
    n_b = (CONV_B_WIDTH - 1) * nb
    slab_rows = nb * SLAB_PITCH
    hbm = pl.BlockSpec(memory_space=pl.ANY)
    out_shape = (
        jax.ShapeDtypeStruct((nb, seq, D_MODEL), F32),
        jax.ShapeDtypeStruct((n_a, W_A), F32),
        jax.ShapeDtypeStruct((n_b, W_B), F32),
        jax.ShapeDtypeStruct((nb, W_B), F32),
    )
    const_out = lambda shape: pl.BlockSpec(shape, lambda i: (0, 0))
    return pl.pallas_call(
        functools.partial(_prompt_kernel, nb),
        grid=(seq // TT,),
        in_specs=[hbm, hbm] + _weight_specs(),
        out_specs=(hbm, const_out((n_a, W_A)), const_out((n_b, W_B)), const_out((nb, W_B))),
        out_shape=out_shape,
        scratch_shapes=[
            pltpu.VMEM((rows, D_MODEL), BF16),
            pltpu.VMEM((NC_GLU, rows, MXU_N), F32),
            pltpu.VMEM((NC_REST, HIST_B * nb + rows, MXU_N), F32),
            pltpu.VMEM(((HIST_A + TT) * nb, W_A), F32),
            pltpu.VMEM((rows, D_MODEL), F32),
            pltpu.VMEM((rows, W_B), F32),
            pltpu.VMEM((rows, W_A), BF16),
            pltpu.VMEM((rows, W_B), BF16),
            pltpu.VMEM((N_B_HEADS, rows, HEAD_B), BF16),
            pltpu.VMEM((nb, W_B), F32),
            pltpu.VMEM(((CONV_A_WIDTH + CONV_PAIRS) * nb, W_A), F32),
            pltpu.VMEM((2, D_MODEL // LANES, slab_rows, LANES), F32),
            pltpu.VMEM((2, PLE_DIM // LANES, slab_rows, LANES), F32),
            pltpu.VMEM((D_MODEL // LANES, slab_rows, LANES), F32),
            pltpu.SemaphoreType.DMA((2,)),
            pltpu.SemaphoreType.DMA((1,)),
        ],
        compiler_params=pltpu.CompilerParams(
            dimension_semantics=("arbitrary",), vmem_limit_bytes=VMEM_LIMIT_BYTES),
        name="prompt_layer",
    )(x, p, *weights)


def _sample_call(x_s, p_s, sa, sb_t, h0, weights):
    n = x_s.shape[0]
    ka = CONV_A_WIDTH - 1
    kb = CONV_B_WIDTH - 1
    out_shape = (
        jax.ShapeDtypeStruct((n, D_MODEL), F32),
        jax.ShapeDtypeStruct((ka, n, W_A), F32),
        jax.ShapeDtypeStruct((kb, n, W_B), F32),
        jax.ShapeDtypeStruct((n, W_B), F32),
    )
    full2 = lambda shape: pl.BlockSpec(shape, lambda i: (0, 0))
    full3 = lambda shape: pl.BlockSpec(shape, lambda i: (0, 0, 0))
    return pl.pallas_call(
        _sample_kernel,
        grid=(n // SAMPLE_BT,),
        in_specs=[full2((n, D_MODEL)), full2((n, PLE_DIM)),
                  pl.BlockSpec((ka, SAMPLE_BT, W_A), lambda i: (0, i, 0)),
                  full3((kb, n, W_B)), full2((n, W_B))] + _weight_specs(),
        out_specs=(full2((n, D_MODEL)),
                   pl.BlockSpec((ka, SAMPLE_BT, W_A), lambda i: (0, i, 0)),
                   full3((kb, n, W_B)), full2((n, W_B))),
        out_shape=out_shape,
        scratch_shapes=[
            pltpu.VMEM((n, D_MODEL), BF16),
            pltpu.VMEM((n, W_IN - 2 * W_A), F32),
            pltpu.VMEM((n, W_A), F32),
            pltpu.VMEM((n, W_A), F32),
        ],
        compiler_params=pltpu.CompilerParams(
            dimension_semantics=("arbitrary",), vmem_limit_bytes=VMEM_LIMIT_BYTES),
        name="sample_layer",
    )(x_s, p_s, sa, sb_t, h0, *weights)


def kernel(x_prompt, x_sample, p_prompt, p_sample, state_conv_a, state_conv_b, state_h, g_norm, w_in, w_dw_a, b_dw_a, ln_g, ln_b, w_conv_b, b_conv_b, w_r, b_r, w_i, b_i, lam, w_out, w_pe, w_pg, g_final):
    depth = g_norm.shape[0]
    assert depth == 1, "single-layer step"
    nb, seq, _ = x_prompt.shape
    assert nb == SUBLANES and seq % TT == 0 and TT >= HIST_A
    row = lambda a: a.reshape(1, -1)
    win16, wout16, wpe16, wpg16 = _cast_weights(w_in[0], w_out[0], w_pe[0], w_pg[0])
    weights = (
        row(g_norm[0]), win16,
        w_dw_a[0], row(b_dw_a[0]), row(ln_g[0]), row(ln_b[0]),
        w_conv_b[0], row(b_conv_b[0]),
        jnp.concatenate([w_r[0], w_i[0]], axis=-1).astype(BF16), row(b_r[0]), row(b_i[0]), row(lam[0]),
        wout16, wpe16, wpg16, row(g_final),
    )

    y_prompt, na_t, nbuf_t, nh_p = _prompt_call(x_prompt, p_prompt[0], weights)
    untime = lambda a, k: jnp.swapaxes(a.reshape(k, nb, a.shape[-1]), 0, 1)
    na_p = untime(na_t, CONV_A_WIDTH - 1)[None]
    nb_p = untime(nbuf_t, CONV_B_WIDTH - 1)[None]

    n_s = x_sample.shape[0]
    assert x_sample.shape[1] == 1 and n_s % SAMPLE_BT == 0
    y_s, na_s, nbuf_s, nh_s = _sample_call(
        x_sample.reshape(n_s, D_MODEL), p_sample[0].reshape(n_s, PLE_DIM), jnp.swapaxes(state_conv_a[0], 0, 1),
        jnp.swapaxes(state_conv_b[0], 0, 1), state_h[0], weights)
    return (y_prompt, y_s.reshape(n_s, 1, D_MODEL), na_p, nb_p, nh_p[None],
            jnp.swapaxes(na_s, 0, 1)[None], jnp.swapaxes(nbuf_s, 0, 1)[None], nh_s[None])
```

```python
import functools

import jax
import jax.numpy as jnp
from jax import lax
from jax.experimental import pallas as pl
from jax.experimental.pallas import tpu as pltpu

D_MODEL = 1024
W_A = D_MODEL
W_B = D_MODEL
N_B_HEADS = 8
HEAD_B = W_B // N_B_HEADS
W_IN = 3 * W_A + 2 * W_B
CONV_A_WIDTH = 31
CONV_B_WIDTH = 4
RG_C = 8.0
PLE_DIM = 256
EPS = 1e-6

SUBLANES = 8
LANES = 128
MXU_N = 256
VMEM_LIMIT_BYTES = 60000 * 1024

TT = 64
HIST_A = 32
HIST_B = 4
SLAB_PITCH = TT + 8
ROW_TILE = 32
CONV_T = 8
CONV_PAIRS = (CONV_A_WIDTH - 1) // 2
SAMPLE_BT = 32
CAST_STEPS = 8

NC_D = D_MODEL // MXU_N
NC_GLU = 2 * W_A // MXU_N
NC_REST = (W_IN - 2 * W_A) // MXU_N
CH_GATE_A = 0
CH_BX = W_A // MXU_N
CH_GATE_B = (W_A + W_B) // MXU_N

F32 = jnp.float32
BF16 = jnp.bfloat16


def _sigmoid(x):
    return 0.5 * jnp.tanh(0.5 * x) + 0.5


def _silu(x):
    hx = 0.5 * x
    return hx * jnp.tanh(hx) + hx


def _dot(a, b):
    return jnp.dot(a, b, preferred_element_type=F32)


def _rmsnorm(x, g):
    ms = jnp.mean(x * x, axis=-1, keepdims=True)
    return x * lax.rsqrt(ms + EPS) * g


def _layernorm(x, g, b):
    mu = jnp.mean(x, axis=-1, keepdims=True)
    xc = x - mu
    var = jnp.mean(xc * xc, axis=-1, keepdims=True)
    return xc * lax.rsqrt(var + EPS) * g + b


def _rglru_coeffs(r, ig, xb, c_lam):
    log_a = c_lam * r
    a = jnp.exp(log_a)
    th = jnp.tanh(log_a)
    y = -2.0 * th / (1.0 - th)
    root = jnp.where(y == 0.0, 0.0, y * lax.rsqrt(y))
    bx = root * (ig * xb)
    return a, bx


def _aligned(v, m):
    return v if isinstance(v, int) else pl.multiple_of(v, m)


def _chunk_cols(c):
    return slice(c * MXU_N, (c + 1) * MXU_N)


def _lanes_of_chunks(ref, first, rows_sl):
    return jnp.concatenate([ref[first + c, rows_sl, :] for c in range(NC_D)], axis=-1)


def _prompt_kernel(nb, x_hbm, p_hbm, gn_ref, win_ref, wdw_ref, bdw_ref, lng_ref, lnb_ref,
                   wcb_ref, bcb_ref, wri_ref, br_ref, bi_ref, lam_ref, wout_ref, wpe_ref,
                   wpg_ref, gf_ref,
                   y_hbm, na_ref, nbuf_ref, nh_ref,
                   u_s, z_s, zb_s, vp_s, cv_s, a_s, mixa_s, mixb_s, xb16_s, h_s, wb_s,
                   xs_s, ps_s, ys_s, sem_in, sem_out):
    rows = TT * nb
    ha = HIST_A * nb
    hb = HIST_B * nb
    step = pl.program_id(0)
    n_steps = pl.num_programs(0)
    n_tiles = rows // ROW_TILE
    cur = slice(hb, hb + rows)
    heads = [slice(h * HEAD_B, (h + 1) * HEAD_B) for h in range(N_B_HEADS)]
    slot = step % 2

    def in_copies(s, sl):
        t0 = pl.multiple_of(s * TT, TT)
        cps = []
        for b in range(nb):
            dst_rows = pl.ds(b * SLAB_PITCH, TT)
            for j in range(D_MODEL // LANES):
                cps.append(pltpu.make_async_copy(
                    x_hbm.at[b, pl.ds(t0, TT), pl.ds(j * LANES, LANES)],
                    xs_s.at[sl, j, dst_rows, :], sem_in.at[sl]))
            for j in range(PLE_DIM // LANES):
                cps.append(pltpu.make_async_copy(
                    p_hbm.at[b, pl.ds(t0, TT), pl.ds(j * LANES, LANES)],
                    ps_s.at[sl, j, dst_rows, :], sem_in.at[sl]))
        return cps

    def out_copies(s):
        t0 = pl.multiple_of(s * TT, TT)
        return [pltpu.make_async_copy(
            ys_s.at[j, pl.ds(b * SLAB_PITCH, TT), :],
            y_hbm.at[b, pl.ds(t0, TT), pl.ds(j * LANES, LANES)], sem_out.at[0])
            for b in range(nb) for j in range(D_MODEL // LANES)]

    @pl.when(step == 0)
    def _first_fetch():
        for cp in in_copies(0, 0):
            cp.start()

    @pl.when(step + 1 < n_steps)
    def _prefetch():
        for cp in in_copies(step + 1, 1 - slot):
            cp.start()

    @pl.when(step == 0)
    def _init():
        vp_s[0:ha, :] = jnp.zeros((ha, W_A), F32)
        for c in range(NC_D):
            zb_s[CH_BX + c, rows:rows + hb, :] = jnp.zeros((hb, MXU_N), F32)
        h_s[...] = jnp.zeros_like(h_s)
        for k in range(CONV_A_WIDTH):
            wb_s[k * nb:(k + 1) * nb, :] = jnp.broadcast_to(wdw_ref[k:k + 1, :], (nb, W_A))
        for i in range(CONV_PAIRS):
            k = CONV_A_WIDTH + i
            wb_s[k * nb:(k + 1) * nb, :] = jnp.broadcast_to(
                wdw_ref[2 * i:2 * i + 1, :] + wdw_ref[2 * i + 1:2 * i + 2, :], (nb, W_A))

    @pl.when(step > 0)
    def _carry():
        vp_s[0:ha, :] = vp_s[rows:rows + ha, :]

    def tiled_loop(n_iter, tile_body, host_body=None):
        per_iter = n_tiles // n_iter

        def body(it, c):
            if host_body is not None:
                host_body(it)
            for q in range(per_iter):
                tile_body(_aligned((it * per_iter + q) * ROW_TILE, ROW_TILE))
            return c
        if host_body is not None:
            for it in range(n_iter):
                body(it, 0)
        else:
            lax.fori_loop(0, n_iter, body, 0)

    for cp in in_copies(step, slot):
        cp.wait()

    def time_rows(slab, groups, r0):
        t = lax.shift_right_logical(r0, SUBLANES.bit_length() - 1)
        return jnp.concatenate(
            [jnp.concatenate([slab[slot, j, pl.ds(t + q, nb, stride=SLAB_PITCH), :] for j in range(groups)],
                             axis=-1) for q in range(ROW_TILE // nb)], axis=0)

    def norm_in(r0):
        rs = pl.ds(r0, ROW_TILE)
        x = time_rows(xs_s, D_MODEL // LANES, r0)
        u_s[rs, :] = _rmsnorm(x, gn_ref[...]).astype(BF16)
    tiled_loop(1, norm_in)

    for c in range(NC_D):
        z_s[c] = _sigmoid(_dot(u_s[...], win_ref[:, _chunk_cols(NC_D + c)]))
        vp_s[ha:ha + rows, _chunk_cols(c)] = _dot(u_s[...], win_ref[:, _chunk_cols(c)]) * z_s[c]

    def carry_bx(c, carry):
        zb_s[CH_BX + c, 0:hb, :] = zb_s[CH_BX + c, rows:rows + hb, :]
        return carry
    lax.fori_loop(0, NC_D, carry_bx, 0)

    half = CONV_T // 2
    p_handoff = {}

    def conv_piece(blk, j):
        base = blk * (CONV_T * nb)
        ls = slice(j * LANES, (j + 1) * LANES)
        xp_cache, tap_cache = {}, {}

        def xp(d):
            if d not in xp_cache:
                off = (d + HIST_A - (CONV_A_WIDTH - 1)) * nb
                xp_cache[d] = vp_s[base + off:base + off + nb, ls]
            return xp_cache[d]

        def tap(idx):
            if idx not in tap_cache:
                tap_cache[idx] = wb_s[idx * nb:(idx + 1) * nb, ls]
            return tap_cache[idx]

        def add(acc, term):
            return term if acc is None else acc + term

        p = [None] * (half + 1)
        q = [None] * half
        u = [None] * half
        if blk > 0:
            p[0] = p_handoff[j]
        for n in range(half + CONV_PAIRS):
            xe = xp(2 * n)
            last = n == half + CONV_PAIRS - 1
            xo = None if last else xp(2 * n + 1)
            xs = None if last else xo + xp(2 * n + 2)
            for m in range(half + 1):
                i = n - m
                if not 0 <= i < CONV_PAIRS:
                    continue
                if m > 0 or blk == 0:
                    p[m] = add(p[m], tap(2 * i) * xe)
                if m < half:
                    q[m] = add(q[m], tap(2 * i + 1) * xo)
                    u[m] = add(u[m], tap(CONV_A_WIDTH + i) * xs)
        p_handoff[j] = p[half]
        bias = jnp.broadcast_to(bdw_ref[:, ls], (nb, LANES))
        w_last = tap(CONV_A_WIDTH - 1)
        for m in range(half):
            even = (p[m] + q[m]) + (bias + w_last * xp(2 * m + CONV_A_WIDTH - 1))
            odd = (u[m] - p[m + 1] - q[m]) + (bias + w_last * xp(2 * m + CONV_A_WIDTH))
            cv_s[base + 2 * m * nb:base + (2 * m + 1) * nb, ls] = even
            cv_s[base + (2 * m + 1) * nb:base + (2 * m + 2) * nb, ls] = odd

    def in_dot(ch):
        z = _dot(u_s[...], win_ref[:, _chunk_cols(NC_GLU + ch)])
        is_gate = not CH_BX <= ch < CH_GATE_B
        zb_s[ch, cur, :] = _silu(z) if is_gate else z

    pieces = [(blk, j) for j in range(W_A // LANES) for blk in range(TT // CONV_T)]
    next_dot = 0
    for n, piece in enumerate(pieces):
        want = -(-(n + 1) * NC_REST // len(pieces))
        while next_dot < want:
            in_dot(next_dot)
            next_dot += 1
        conv_piece(*piece)

    def conv_b(r0):
        rs = pl.ds(r0, ROW_TILE)
        for h in range(N_B_HEADS):
            ls = slice(h * HEAD_B, (h + 1) * HEAD_B)
            ch = CH_BX + (h * HEAD_B) // MXU_N
            lo = (h * HEAD_B) % MXU_N
            acc = jnp.broadcast_to(bcb_ref[:, ls], (ROW_TILE, HEAD_B))
            for k in range(CONV_B_WIDTH):
                off = (k + HIST_B - (CONV_B_WIDTH - 1)) * nb
                acc = acc + wcb_ref[k:k + 1, ls] * zb_s[ch, pl.ds(r0 + off, ROW_TILE), lo:lo + HEAD_B]
            a_s[rs, ls] = acc
            xb16_s[h, rs, :] = acc.astype(BF16)
    tiled_loop(4, conv_b)

    def gate_dot(it):
        for q in range(2):
            h = it * 2 + q
            bias = jnp.concatenate([br_ref[:, heads[h]], bi_ref[:, heads[h]]], axis=-1)
            z_s[h] = _sigmoid(_dot(xb16_s[h], wri_ref[h]) + bias)

    def mix_a(r0):
        rs = pl.ds(r0, ROW_TILE)
        v = _layernorm(cv_s[rs, :], lng_ref[...], lnb_ref[...])
        gate = _lanes_of_chunks(zb_s, CH_GATE_A, pl.ds(hb + r0, ROW_TILE))
        mixa_s[rs, :] = (_silu(v) * gate).astype(BF16)
    tiled_loop(N_B_HEADS // 2, mix_a, gate_dot)

    c_lam = -RG_C * jax.nn.softplus(-lam_ref[...])

    def out_a_dot(c):
        zb_s[CH_GATE_A + c, cur, :] = _dot(mixa_s[...], wout_ref[0:W_A, _chunk_cols(c)])

    h_run = [h_s[:, ls] for ls in heads]

    def coeffs(r0):
        rs = pl.ds(r0, ROW_TILE)
        for h, ls in enumerate(heads):
            a, bx = _rglru_coeffs(z_s[h, rs, 0:HEAD_B], z_s[h, rs, HEAD_B:2 * HEAD_B], a_s[rs, ls], c_lam[:, ls])
            hs = []
            for q in range(ROW_TILE // nb):
                tq = slice(q * nb, (q + 1) * nb)
                h_run[h] = a[tq] * h_run[h] + bx[tq]
                hs.append(h_run[h])
            ch, lo = CH_GATE_B + (h * HEAD_B) // MXU_N, (h * HEAD_B) % MXU_N
            gate = zb_s[ch, hb + r0:hb + r0 + ROW_TILE, lo:lo + HEAD_B]
            mixb_s[rs, ls] = (jnp.concatenate(hs, axis=0) * gate).astype(BF16)
    tiled_loop(NC_D, coeffs, out_a_dot)
    for h, ls in enumerate(heads):
        h_s[:, ls] = h_run[h]

    def slab_cols(slab, c):
        lanes = []
        for j in range(c * (MXU_N // LANES), (c + 1) * (MXU_N // LANES)):
            lanes.append(jnp.concatenate(
                [slab[slot, j, pl.ds(t, nb, stride=SLAB_PITCH), :] for t in range(TT)], axis=0))
        return jnp.concatenate(lanes, axis=-1)

    for c in range(NC_D):
        h1 = (_dot(mixb_s[...], wout_ref[W_A:W_A + W_B, _chunk_cols(c)])
              + zb_s[CH_GATE_A + c, cur, :] + slab_cols(xs_s, c))
        cv_s[:, _chunk_cols(c)] = h1
        u_s[:, _chunk_cols(c)] = h1.astype(BF16)

    p16 = slab_cols(ps_s, 0).astype(BF16)
    for c in range(NC_D):
        z_s[c] = _sigmoid(_dot(u_s[...], wpg_ref[:, _chunk_cols(c)]))
        cv_s[:, _chunk_cols(c)] = cv_s[:, _chunk_cols(c)] + _dot(p16, wpe_ref[:, _chunk_cols(c)]) * z_s[c]

    @pl.when(step > 0)
    def _drain_prev():
        for cp in out_copies(step - 1):
            cp.wait()

    def finish(r0):
        rs = pl.ds(r0, ROW_TILE)
        y = _rmsnorm(cv_s[rs, :], gf_ref[...])
        t = lax.shift_right_logical(r0, SUBLANES.bit_length() - 1)
        for q in range(ROW_TILE // nb):
            for j in range(D_MODEL // LANES):
                ys_s[j, pl.ds(t + q, nb, stride=SLAB_PITCH), :] = y[q * nb:(q + 1) * nb, j * LANES:(j + 1) * LANES]
    tiled_loop(1, finish)

    for cp in out_copies(step):
        cp.start()

    @pl.when(step == n_steps - 1)
    def _drain_last():
        for cp in out_copies(step):
            cp.wait()

    @pl.when(step == n_steps - 1)
    def _state_out():
        na = (CONV_A_WIDTH - 1) * nb
        nbb = (CONV_B_WIDTH - 1) * nb
        na_ref[...] = vp_s[ha + rows - na:ha + rows, :]
        for c in range(NC_D):
            nbuf_ref[:, c * MXU_N:(c + 1) * MXU_N] = zb_s[CH_BX + c, hb + rows - nbb:hb + rows, :]
        nh_ref[...] = h_s[...]


def _sample_kernel(x_ref, p_ref, sa_ref, sb_ref, h0_ref, gn_ref, win_ref, wdw_ref, bdw_ref,
                   lng_ref, lnb_ref, wcb_ref, bcb_ref, wri_ref, br_ref, bi_ref, lam_ref,
                   wout_ref, wpe_ref, wpg_ref, gf_ref,
                   y_ref, na_ref, nbuf_ref, nh_ref,
                   u_s, z_s, v_s, cvp_s):
    step = pl.program_id(0)
    ka = CONV_A_WIDTH - 1
    kb = CONV_B_WIDTH - 1

    @pl.when(step == 0)
    def _project():
        u = _rmsnorm(x_ref[...], gn_ref[...]).astype(BF16)
        u_s[...] = u
        za = _dot(u, win_ref[:, 0:2 * W_A])
        v_s[...] = za[:, 0:W_A] * _sigmoid(za[:, W_A:2 * W_A])
        z_s[...] = _dot(u, win_ref[:, 2 * W_A:W_IN])

    seqs = pl.ds(pl.multiple_of(step * SAMPLE_BT, SAMPLE_BT), SAMPLE_BT)
    acc = wdw_ref[0:1, :] * sa_ref[0]
    for k in range(1, ka):
        acc = acc + wdw_ref[k:k + 1, :] * sa_ref[k]
        na_ref[k - 1] = sa_ref[k]
    cvp_s[seqs, :] = acc
    na_ref[ka - 1] = v_s[seqs, :]

    @pl.when(step == pl.num_programs(0) - 1)
    def _rest():
        v = v_s[...]
        cv = cvp_s[...] + wdw_ref[ka:ka + 1, :] * v + bdw_ref[...]
        v = _silu(_layernorm(cv, lng_ref[...], lnb_ref[...])) * _silu(z_s[:, 0:W_A])
        mix_a = v.astype(BF16)

        b_x = z_s[:, W_A:W_A + W_B]
        xb = wcb_ref[kb:kb + 1, :] * b_x + bcb_ref[...]
        for k in range(kb):
            xb = xb + wcb_ref[k:k + 1, :] * sb_ref[k]
        for k in range(kb - 1):
            nbuf_ref[k] = sb_ref[k + 1]
        nbuf_ref[kb - 1] = b_x

        xb16 = xb.astype(BF16)
        c_lam = -RG_C * jax.nn.softplus(-lam_ref[...])
        a_parts, b_parts = [], []
        for h in range(N_B_HEADS):
            ls = slice(h * HEAD_B, (h + 1) * HEAD_B)
            g = _dot(xb16[:, ls], wri_ref[h])
            a, bx = _rglru_coeffs(_sigmoid(g[:, 0:HEAD_B] + br_ref[:, ls]),
                                  _sigmoid(g[:, HEAD_B:2 * HEAD_B] + bi_ref[:, ls]), xb[:, ls], c_lam[:, ls])
            a_parts.append(a)
            b_parts.append(bx)
        hn = jnp.concatenate(a_parts, axis=-1) * h0_ref[...] + jnp.concatenate(b_parts, axis=-1)
        nh_ref[...] = hn
        mix_b = (hn * _silu(z_s[:, W_A + W_B:W_A + 2 * W_B])).astype(BF16)

        mix = _dot(mix_a, wout_ref[0:W_A, :]) + _dot(mix_b, wout_ref[W_A:W_A + W_B, :])
        h1 = x_ref[...] + mix
        pe = _dot(p_ref[...].astype(BF16), wpe_ref[...]) * _sigmoid(_dot(h1.astype(BF16), wpg_ref[...]))
        y_ref[...] = _rmsnorm(h1 + pe, gf_ref[...])


def _cast_kernel(*refs):
    n = len(refs) // 2
    for src, dst in zip(refs[:n], refs[n:]):
        dst[...] = src[...].astype(BF16)


def _cast_weights(*ws):
    spec = lambda w: pl.BlockSpec((w.shape[0] // CAST_STEPS, w.shape[1]), lambda i: (i, 0))
    return pl.pallas_call(
        _cast_kernel,
        grid=(CAST_STEPS,),
        in_specs=[spec(w) for w in ws],
        out_specs=[spec(w) for w in ws],
        out_shape=[jax.ShapeDtypeStruct(w.shape, BF16) for w in ws],
        compiler_params=pltpu.CompilerParams(dimension_semantics=("arbitrary",)),
        name="cast_weights",
    )(*ws)


def _const_spec(shape):
    zeros = (0,) * len(shape)
    return pl.BlockSpec(shape, lambda i: zeros, pipeline_mode=pl.Buffered(1))


def _weight_specs():
    return [
        _const_spec((1, D_MODEL)),
        _const_spec((D_MODEL, W_IN)),
        _const_spec((CONV_A_WIDTH, W_A)),
        _const_spec((1, W_A)),
        _const_spec((1, W_A)),
        _const_spec((1, W_A)),
        _const_spec((CONV_B_WIDTH, W_B)),
        _const_spec((1, W_B)),
        _const_spec((N_B_HEADS, HEAD_B, 2 * HEAD_B)),
        _const_spec((1, W_B)),
        _const_spec((1, W_B)),
        _const_spec((1, W_B)),
        _const_spec((W_A + W_B, D_MODEL)),
        _const_spec((PLE_DIM, D_MODEL)),
        _const_spec((D_MODEL, D_MODEL)),
        _const_spec((1, D_MODEL)),
    ]


def _prompt_call(x, p, weights):
    nb, seq, _ = x.shape
    rows = TT * nb
    n_a = (CONV_A_WIDTH - 1) * nb
    n_b = (CONV_B_WIDTH - 1) * nb
    slab_rows = nb * SLAB_PITCH
    hbm = pl.BlockSpec(memory_space=pl.ANY)
    out_shape = (
        jax.ShapeDtypeStruct((nb, seq, D_MODEL), F32),
        jax.ShapeDtypeStruct((n_a, W_A), F32),
        jax.ShapeDtypeStruct((n_b, W_B), F32),
        jax.ShapeDtypeStruct((nb, W_B), F32),
    )
    const_out = lambda shape: pl.BlockSpec(shape, lambda i: (0, 0))
    return pl.pallas_call(
        functools.partial(_prompt_kernel, nb),
        grid=(seq // TT,),
        in_specs=[hbm, hbm] + _weight_specs(),
        out_specs=(hbm, const_out((n_a, W_A)), const_out((n_b, W_B)), const_out((nb, W_B))),
        out_shape=out_shape,
        scratch_shapes=[
            pltpu.VMEM((rows, D_MODEL), BF16),
            pltpu.VMEM((NC_GLU, rows, MXU_N), F32),
            pltpu.VMEM((NC_REST, HIST_B * nb + rows, MXU_N), F32),
            pltpu.VMEM(((HIST_A + TT) * nb, W_A), F32),
            pltpu.VMEM((rows, D_MODEL), F32),
            pltpu.VMEM((rows, W_B), F32),
            pltpu.VMEM((rows, W_A), BF16),
            pltpu.VMEM((rows, W_B), BF16),
            pltpu.VMEM((N_B_HEADS, rows, HEAD_B), BF16),
            pltpu.VMEM((nb, W_B), F32),
            pltpu.VMEM(((CONV_A_WIDTH + CONV_PAIRS) * nb, W_A), F32),
            pltpu.VMEM((2, D_MODEL // LANES, slab_rows, LANES), F32),
            pltpu.VMEM((2, PLE_DIM // LANES, slab_rows, LANES), F32),
            pltpu.VMEM((D_MODEL // LANES, slab_rows, LANES), F32),
            pltpu.SemaphoreType.DMA((2,)),
            pltpu.SemaphoreType.DMA((1,)),
        ],
        compiler_params=pltpu.CompilerParams(
            dimension_semantics=("arbitrary",), vmem_limit_bytes=VMEM_LIMIT_BYTES),
        name="prompt_layer",
    )(x, p, *weights)


def _sample_call(x_s, p_s, sa, sb_t, h0, weights):
    n = x_s.shape[0]
    ka = CONV_A_WIDTH - 1
    kb = CONV_B_WIDTH - 1
    out_shape = (
        jax.ShapeDtypeStruct((n, D_MODEL), F32),
        jax.ShapeDtypeStruct((ka, n, W_A), F32),
        jax.ShapeDtypeStruct((kb, n, W_B), F32),
        jax.ShapeDtypeStruct((n, W_B), F32),
    )
    full2 = lambda shape: pl.BlockSpec(shape, lambda i: (0, 0))
    full3 = lambda shape: pl.BlockSpec(shape, lambda i: (0, 0, 0))
    return pl.pallas_call(
        _sample_kernel,
        grid=(n // SAMPLE_BT,),
        in_specs=[full2((n, D_MODEL)), full2((n, PLE_DIM)),
                  pl.BlockSpec((ka, SAMPLE_BT, W_A), lambda i: (0, i, 0)),
                  full3((kb, n, W_B)), full2((n, W_B))] + _weight_specs(),
        out_specs=(full2((n, D_MODEL)),
                   pl.BlockSpec((ka, SAMPLE_BT, W_A), lambda i: (0, i, 0)),
                   full3((kb, n, W_B)), full2((n, W_B))),
        out_shape=out_shape,
        scratch_shapes=[
            pltpu.VMEM((n, D_MODEL), BF16),
            pltpu.VMEM((n, W_IN - 2 * W_A), F32),
            pltpu.VMEM((n, W_A), F32),
            pltpu.VMEM((n, W_A), F32),
        ],
        compiler_params=pltpu.CompilerParams(
            dimension_semantics=("arbitrary",), vmem_limit_bytes=VMEM_LIMIT_BYTES),
        name="sample_layer",
    )(x_s, p_s, sa, sb_t, h0, *weights)


def kernel(x_prompt, x_sample, p_prompt, p_sample, state_conv_a, state_conv_b, state_h, g_norm, w_in, w_dw_a, b_dw_a, ln_g, ln_b, w_conv_b, b_conv_b, w_r, b_r, w_i, b_i, lam, w_out, w_pe, w_pg, g_final):
    depth = g_norm.shape[0]
    assert depth == 1, "single-layer step"
    nb, seq, _ = x_prompt.shape
    assert nb == SUBLANES and seq % TT == 0 and TT >= HIST_A
    row = lambda a: a.reshape(1, -1)
    win16, wout16, wpe16, wpg16 = _cast_weights(w_in[0], w_out[0], w_pe[0], w_pg[0])
    weights = (
        row(g_norm[0]), win16,
        w_dw_a[0], row(b_dw_a[0]), row(ln_g[0]), row(ln_b[0]),
        w_conv_b[0], row(b_conv_b[0]),
        jnp.concatenate([w_r[0], w_i[0]], axis=-1).astype(BF16), row(b_r[0]), row(b_i[0]), row(lam[0]),
        wout16, wpe16, wpg16, row(g_final),
    )

    y_prompt, na_t, nbuf_t, nh_p = _prompt_call(x_prompt, p_prompt[0], weights)
    untime = lambda a, k: jnp.swapaxes(a.reshape(k, nb, a.shape[-1]), 0, 1)
    na_p = untime(na_t, CONV_A_WIDTH - 1)[None]
    nb_p = untime(nbuf_t, CONV_B_WIDTH - 1)[None]

    n_s = x_sample.shape[0]
    assert x_sample.shape[1] == 1 and n_s % SAMPLE_BT == 0
    y_s, na_s, nbuf_s, nh_s = _sample_call(
        x_sample.reshape(n_s, D_MODEL), p_sample[0].reshape(n_s, PLE_DIM), jnp.swapaxes(state_conv_a[0], 0, 1),
        jnp.swapaxes(state_conv_b[0], 0, 1), state_h[0], weights)
    return (y_prompt, y_s.reshape(n_s, 1, D_MODEL), na_p, nb_p, nh_p[None],
            jnp.swapaxes(na_s, 0, 1)[None], jnp.swapaxes(nbuf_s, 0, 1)[None], nh_s[None])
```

```python
import functools

import jax
import jax.numpy as jnp
from jax import lax
from jax.experimental import pallas as pl
from jax.experimental.pallas import tpu as pltpu

D_MODEL = 1024
W_A = D_MODEL
W_B = D_MODEL
N_B_HEADS = 8
HEAD_B = W_B // N_B_HEADS
W_IN = 3 * W_A + 2 * W_B
CONV_A_WIDTH = 31
CONV_B_WIDTH = 4
RG_C = 8.0
PLE_DIM = 256
EPS = 1e-6

SUBLANES = 8
LANES = 128
MXU_N = 256
VMEM_LIMIT_BYTES = 60000 * 1024

TT = 64
HIST_A = 32
HIST_B = 4
SLAB_PITCH = TT + 8
ROW_TILE = 32
CONV_T = 8
CONV_PAIRS = (CONV_A_WIDTH - 1) // 2
SAMPLE_BT = 32
CAST_STEPS = 8

NC_D = D_MODEL // MXU_N
NC_GLU = 2 * W_A // MXU_N
NC_REST = (W_IN - 2 * W_A) // MXU_N
CH_GATE_A = 0
CH_BX = W_A // MXU_N
CH_GATE_B = (W_A + W_B) // MXU_N

F32 = jnp.float32
BF16 = jnp.bfloat16


def _sigmoid(x):
    return 0.5 * jnp.tanh(0.5 * x) + 0.5


def _silu(x):
    hx = 0.5 * x
    return hx * jnp.tanh(hx) + hx


def _dot(a, b):
    return jnp.dot(a, b, preferred_element_type=F32)


def _rmsnorm(x, g):
    ms = jnp.mean(x * x, axis=-1, keepdims=True)
    return x * lax.rsqrt(ms + EPS) * g


def _layernorm(x, g, b):
    mu = jnp.mean(x, axis=-1, keepdims=True)
    xc = x - mu
    var = jnp.mean(xc * xc, axis=-1, keepdims=True)
    return xc * lax.rsqrt(var + EPS) * g + b


def _rglru_coeffs(r, ig, xb, c_lam):
    log_a = c_lam * r
    a = jnp.exp(log_a)
    th = jnp.tanh(log_a)
    y = -2.0 * th / (1.0 - th)
    root = jnp.where(y == 0.0, 0.0, y * lax.rsqrt(y))
    bx = root * (ig * xb)
    return a, bx


def _aligned(v, m):
    return v if isinstance(v, int) else pl.multiple_of(v, m)


def _chunk_cols(c):
    return slice(c * MXU_N, (c + 1) * MXU_N)


def _lanes_of_chunks(ref, first, rows_sl):
    return jnp.concatenate([ref[first + c, rows_sl, :] for c in range(NC_D)], axis=-1)


def _prompt_kernel(nb, x_hbm, p_hbm, gn_ref, win_ref, wdw_ref, bdw_ref, lng_ref, lnb_ref,
                   wcb_ref, bcb_ref, wri_ref, br_ref, bi_ref, lam_ref, wout_ref, wpe_ref,
                   wpg_ref, gf_ref,
                   y_hbm, na_ref, nbuf_ref, nh_ref,
                   u_s, z_s, zb_s, vp_s, cv_s, a_s, mixa_s, mixb_s, xb16_s, h_s, wb_s,
                   xs_s, ps_s, ys_s, sem_in, sem_out):
    rows = TT * nb
    ha = HIST_A * nb
    hb = HIST_B * nb
    step = pl.program_id(0)
    n_steps = pl.num_programs(0)
    n_tiles = rows // ROW_TILE
    cur = slice(hb, hb + rows)
    heads = [slice(h * HEAD_B, (h + 1) * HEAD_B) for h in range(N_B_HEADS)]
    slot = step % 2

    def in_copies(s, sl):
        t0 = pl.multiple_of(s * TT, TT)
        cps = []
        for b in range(nb):
            dst_rows = pl.ds(b * SLAB_PITCH, TT)
            for j in range(D_MODEL // LANES):
                cps.append(pltpu.make_async_copy(
                    x_hbm.at[b, pl.ds(t0, TT), pl.ds(j * LANES, LANES)],
                    xs_s.at[sl, j, dst_rows, :], sem_in.at[sl]))
            for j in range(PLE_DIM // LANES):
                cps.append(pltpu.make_async_copy(
                    p_hbm.at[b, pl.ds(t0, TT), pl.ds(j * LANES, LANES)],
                    ps_s.at[sl, j, dst_rows, :], sem_in.at[sl]))
        return cps

    def out_copies(s):
        t0 = pl.multiple_of(s * TT, TT)
        return [pltpu.make_async_copy(
            ys_s.at[j, pl.ds(b * SLAB_PITCH, TT), :],
            y_hbm.at[b, pl.ds(t0, TT), pl.ds(j * LANES, LANES)], sem_out.at[0])
            for b in range(nb) for j in range(D_MODEL // LANES)]

    def start_all(copies):
        for i, cp in enumerate(copies):
            cp.start(priority=i % 2)

    @pl.when(step == 0)
    def _first_fetch():
        start_all(in_copies(0, 0))

    @pl.when(step + 1 < n_steps)
    def _prefetch():
        start_all(in_copies(step + 1, 1 - slot))

    @pl.when(step == 0)
    def _init():
        vp_s[0:ha, :] = jnp.zeros((ha, W_A), F32)
        for c in range(NC_D):
            zb_s[CH_BX + c, rows:rows + hb, :] = jnp.zeros((hb, MXU_N), F32)
        h_s[...] = jnp.zeros_like(h_s)
        for k in range(CONV_A_WIDTH):
            wb_s[k * nb:(k + 1) * nb, :] = jnp.broadcast_to(wdw_ref[k:k + 1, :], (nb, W_A))
        for i in range(CONV_PAIRS):
            k = CONV_A_WIDTH + i
            wb_s[k * nb:(k + 1) * nb, :] = jnp.broadcast_to(
                wdw_ref[2 * i:2 * i + 1, :] + wdw_ref[2 * i + 1:2 * i + 2, :], (nb, W_A))

    @pl.when(step > 0)
    def _carry():
        vp_s[0:ha, :] = vp_s[rows:rows + ha, :]

    def tiled_loop(n_iter, tile_body, host_body=None):
        per_iter = n_tiles // n_iter

        def body(it, c):
            if host_body is not None:
                host_body(it)
            for q in range(per_iter):
                tile_body(_aligned((it * per_iter + q) * ROW_TILE, ROW_TILE))
            return c
        if host_body is not None:
            for it in range(n_iter):
                body(it, 0)
        else:
            lax.fori_loop(0, n_iter, body, 0)

    for cp in in_copies(step, slot):
        cp.wait()

    def time_rows(slab, groups, r0):
        t = lax.shift_right_logical(r0, SUBLANES.bit_length() - 1)
        return jnp.concatenate(
            [jnp.concatenate([slab[slot, j, pl.ds(t + q, nb, stride=SLAB_PITCH), :] for j in range(groups)],
                             axis=-1) for q in range(ROW_TILE // nb)], axis=0)

    def norm_in(r0):
        rs = pl.ds(r0, ROW_TILE)
        x = time_rows(xs_s, D_MODEL // LANES, r0)
        u_s[rs, :] = _rmsnorm(x, gn_ref[...]).astype(BF16)
    tiled_loop(1, norm_in)

    for c in range(NC_D):
        z_s[c] = _sigmoid(_dot(u_s[...], win_ref[:, _chunk_cols(NC_D + c)]))
        vp_s[ha:ha + rows, _chunk_cols(c)] = _dot(u_s[...], win_ref[:, _chunk_cols(c)]) * z_s[c]

    def carry_bx(c, carry):
        zb_s[CH_BX + c, 0:hb, :] = zb_s[CH_BX + c, rows:rows + hb, :]
        return carry
    lax.fori_loop(0, NC_D, carry_bx, 0)

    half = CONV_T // 2
    p_handoff = {}

    def conv_piece(blk, j):
        base = blk * (CONV_T * nb)
        ls = slice(j * LANES, (j + 1) * LANES)
        xp_cache, tap_cache = {}, {}

        def xp(d):
            if d not in xp_cache:
                off = (d + HIST_A - (CONV_A_WIDTH - 1)) * nb
                xp_cache[d] = vp_s[base + off:base + off + nb, ls]
            return xp_cache[d]

        def tap(idx):
            if idx not in tap_cache:
                tap_cache[idx] = wb_s[idx * nb:(idx + 1) * nb, ls]
            return tap_cache[idx]

        def add(acc, term):
            return term if acc is None else acc + term

        p = [None] * (half + 1)
        q = [None] * half
        u = [None] * half
        if blk > 0:
            p[0] = p_handoff[j]
        for n in range(half + CONV_PAIRS):
            xe = xp(2 * n)
            last = n == half + CONV_PAIRS - 1
            xo = None if last else xp(2 * n + 1)
            xs = None if last else xo + xp(2 * n + 2)
            for m in range(half + 1):
                i = n - m
                if not 0 <= i < CONV_PAIRS:
                    continue
                if m > 0 or blk == 0:
                    p[m] = add(p[m], tap(2 * i) * xe)
                if m < half:
                    q[m] = add(q[m], tap(2 * i + 1) * xo)
                    u[m] = add(u[m], tap(CONV_A_WIDTH + i) * xs)
        p_handoff[j] = p[half]
        bias = jnp.broadcast_to(bdw_ref[:, ls], (nb, LANES))
        w_last = tap(CONV_A_WIDTH - 1)
        for m in range(half):
            even = (p[m] + q[m]) + (bias + w_last * xp(2 * m + CONV_A_WIDTH - 1))
            odd = (u[m] - p[m + 1] - q[m]) + (bias + w_last * xp(2 * m + CONV_A_WIDTH))
            cv_s[base + 2 * m * nb:base + (2 * m + 1) * nb, ls] = even
            cv_s[base + (2 * m + 1) * nb:base + (2 * m + 2) * nb, ls] = odd

    def in_dot(ch):
        z = _dot(u_s[...], win_ref[:, _chunk_cols(NC_GLU + ch)])
        is_gate = not CH_BX <= ch < CH_GATE_B
        zb_s[ch, cur, :] = _silu(z) if is_gate else z

    pieces = [(blk, j) for j in range(W_A // LANES) for blk in range(TT // CONV_T)]
    next_dot = 0
    for n, piece in enumerate(pieces):
        want = -(-(n + 1) * NC_REST // len(pieces))
        while next_dot < want:
            in_dot(next_dot)
            next_dot += 1
        conv_piece(*piece)

    def conv_b(r0):
        rs = pl.ds(r0, ROW_TILE)
        for h in range(N_B_HEADS):
            ls = slice(h * HEAD_B, (h + 1) * HEAD_B)
            ch = CH_BX + (h * HEAD_B) // MXU_N
            lo = (h * HEAD_B) % MXU_N
            acc = jnp.broadcast_to(bcb_ref[:, ls], (ROW_TILE, HEAD_B))
            for k in range(CONV_B_WIDTH):
                off = (k + HIST_B - (CONV_B_WIDTH - 1)) * nb
                acc = acc + wcb_ref[k:k + 1, ls] * zb_s[ch, pl.ds(r0 + off, ROW_TILE), lo:lo + HEAD_B]
            a_s[rs, ls] = acc
            xb16_s[h, rs, :] = acc.astype(BF16)
    tiled_loop(4, conv_b)

    def gate_dot(it):
        for q in range(2):
            h = it * 2 + q
            bias = jnp.concatenate([br_ref[:, heads[h]], bi_ref[:, heads[h]]], axis=-1)
            z_s[h] = _sigmoid(_dot(xb16_s[h], wri_ref[h]) + bias)

    def mix_a(r0):
        rs = pl.ds(r0, ROW_TILE)
        v = _layernorm(cv_s[rs, :], lng_ref[...], lnb_ref[...])
        gate = _lanes_of_chunks(zb_s, CH_GATE_A, pl.ds(hb + r0, ROW_TILE))
        mixa_s[rs, :] = (_silu(v) * gate).astype(BF16)
    tiled_loop(N_B_HEADS // 2, mix_a, gate_dot)

    c_lam = -RG_C * jax.nn.softplus(-lam_ref[...])

    def out_a_dot(c):
        zb_s[CH_GATE_A + c, cur, :] = _dot(mixa_s[...], wout_ref[0:W_A, _chunk_cols(c)])

    h_run = [h_s[:, ls] for ls in heads]

    def coeffs(r0):
        rs = pl.ds(r0, ROW_TILE)
        for h, ls in enumerate(heads):
            a, bx = _rglru_coeffs(z_s[h, rs, 0:HEAD_B], z_s[h, rs, HEAD_B:2 * HEAD_B], a_s[rs, ls], c_lam[:, ls])
            hs = []
            for q in range(ROW_TILE // nb):
                tq = slice(q * nb, (q + 1) * nb)
                h_run[h] = a[tq] * h_run[h] + bx[tq]
                hs.append(h_run[h])
            ch, lo = CH_GATE_B + (h * HEAD_B) // MXU_N, (h * HEAD_B) % MXU_N
            gate = zb_s[ch, hb + r0:hb + r0 + ROW_TILE, lo:lo + HEAD_B]
            mixb_s[rs, ls] = (jnp.concatenate(hs, axis=0) * gate).astype(BF16)
    tiled_loop(NC_D, coeffs, out_a_dot)
    for h, ls in enumerate(heads):
        h_s[:, ls] = h_run[h]

    def slab_cols(slab, c):
        lanes = []
        for j in range(c * (MXU_N // LANES), (c + 1) * (MXU_N // LANES)):
            lanes.append(jnp.concatenate(
                [slab[slot, j, pl.ds(t, nb, stride=SLAB_PITCH), :] for t in range(TT)], axis=0))
        return jnp.concatenate(lanes, axis=-1)

    for c in range(NC_D):
        h1 = (_dot(mixb_s[...], wout_ref[W_A:W_A + W_B, _chunk_cols(c)])
              + zb_s[CH_GATE_A + c, cur, :] + slab_cols(xs_s, c))
        cv_s[:, _chunk_cols(c)] = h1
        u_s[:, _chunk_cols(c)] = h1.astype(BF16)

    p16 = slab_cols(ps_s, 0).astype(BF16)
    for c in range(NC_D):
        z_s[c] = _sigmoid(_dot(u_s[...], wpg_ref[:, _chunk_cols(c)]))
        cv_s[:, _chunk_cols(c)] = cv_s[:, _chunk_cols(c)] + _dot(p16, wpe_ref[:, _chunk_cols(c)]) * z_s[c]

    @pl.when(step > 0)
    def _drain_prev():
        for cp in out_copies(step - 1):
            cp.wait()

    def finish(r0):
        rs = pl.ds(r0, ROW_TILE)
        y = _rmsnorm(cv_s[rs, :], gf_ref[...])
        t = lax.shift_right_logical(r0, SUBLANES.bit_length() - 1)
        for q in range(ROW_TILE // nb):
            for j in range(D_MODEL // LANES):
                ys_s[j, pl.ds(t + q, nb, stride=SLAB_PITCH), :] = y[q * nb:(q + 1) * nb, j * LANES:(j + 1) * LANES]
    tiled_loop(1, finish)

    start_all(out_copies(step))

    @pl.when(step == n_steps - 1)
    def _drain_last():
        for cp in out_copies(step):
            cp.wait()

    @pl.when(step == n_steps - 1)
    def _state_out():
        na = (CONV_A_WIDTH - 1) * nb
        nbb = (CONV_B_WIDTH - 1) * nb
        na_ref[...] = vp_s[ha + rows - na:ha + rows, :]
        for c in range(NC_D):
            nbuf_ref[:, c * MXU_N:(c + 1) * MXU_N] = zb_s[CH_BX + c, hb + rows - nbb:hb + rows, :]
        nh_ref[...] = h_s[...]


def _sample_kernel(x_ref, p_ref, sa_ref, sb_ref, h0_ref, gn_ref, win_ref, wdw_ref, bdw_ref,
                   lng_ref, lnb_ref, wcb_ref, bcb_ref, wri_ref, br_ref, bi_ref, lam_ref,
                   wout_ref, wpe_ref, wpg_ref, gf_ref,
                   y_ref, na_ref, nbuf_ref, nh_ref,
                   u_s, z_s, v_s, cvp_s):
    step = pl.program_id(0)
    ka = CONV_A_WIDTH - 1
    kb = CONV_B_WIDTH - 1

    @pl.when(step == 0)
    def _project():
        u = _rmsnorm(x_ref[...], gn_ref[...]).astype(BF16)
        u_s[...] = u
        za = _dot(u, win_ref[:, 0:2 * W_A])
        v_s[...] = za[:, 0:W_A] * _sigmoid(za[:, W_A:2 * W_A])
        z_s[...] = _dot(u, win_ref[:, 2 * W_A:W_IN])

    seqs = pl.ds(pl.multiple_of(step * SAMPLE_BT, SAMPLE_BT), SAMPLE_BT)
    acc = wdw_ref[0:1, :] * sa_ref[0]
    for k in range(1, ka):
        acc = acc + wdw_ref[k:k + 1, :] * sa_ref[k]
        na_ref[k - 1] = sa_ref[k]
    cvp_s[seqs, :] = acc
    na_ref[ka - 1] = v_s[seqs, :]

    @pl.when(step == pl.num_programs(0) - 1)
    def _rest():
        v = v_s[...]
        cv = cvp_s[...] + wdw_ref[ka:ka + 1, :] * v + bdw_ref[...]
        v = _silu(_layernorm(cv, lng_ref[...], lnb_ref[...])) * _silu(z_s[:, 0:W_A])
        mix_a = v.astype(BF16)

        b_x = z_s[:, W_A:W_A + W_B]
        xb = wcb_ref[kb:kb + 1, :] * b_x + bcb_ref[...]
        for k in range(kb):
            xb = xb + wcb_ref[k:k + 1, :] * sb_ref[k]
        for k in range(kb - 1):
            nbuf_ref[k] = sb_ref[k + 1]
        nbuf_ref[kb - 1] = b_x

        xb16 = xb.astype(BF16)
        c_lam = -RG_C * jax.nn.softplus(-lam_ref[...])
        a_parts, b_parts = [], []
        for h in range(N_B_HEADS):
            ls = slice(h * HEAD_B, (h + 1) * HEAD_B)
            g = _dot(xb16[:, ls], wri_ref[h])
            a, bx = _rglru_coeffs(_sigmoid(g[:, 0:HEAD_B] + br_ref[:, ls]),
                                  _sigmoid(g[:, HEAD_B:2 * HEAD_B] + bi_ref[:, ls]), xb[:, ls], c_lam[:, ls])
            a_parts.append(a)
            b_parts.append(bx)
        hn = jnp.concatenate(a_parts, axis=-1) * h0_ref[...] + jnp.concatenate(b_parts, axis=-1)
        nh_ref[...] = hn
        mix_b = (hn * _silu(z_s[:, W_A + W_B:W_A + 2 * W_B])).astype(BF16)

        mix = _dot(mix_a, wout_ref[0:W_A, :]) + _dot(mix_b, wout_ref[W_A:W_A + W_B, :])
        h1 = x_ref[...] + mix
        pe = _dot(p_ref[...].astype(BF16), wpe_ref[...]) * _sigmoid(_dot(h1.astype(BF16), wpg_ref[...]))
        y_ref[...] = _rmsnorm(h1 + pe, gf_ref[...])


def _cast_kernel(*refs):
    n = len(refs) // 2
    for src, dst in zip(refs[:n], refs[n:]):
        dst[...] = src[...].astype(BF16)


def _cast_weights(*ws):
    spec = lambda w: pl.BlockSpec((w.shape[0] // CAST_STEPS, w.shape[1]), lambda i: (i, 0))
    return pl.pallas_call(
        _cast_kernel,
        grid=(CAST_STEPS,),
        in_specs=[spec(w) for w in ws],
        out_specs=[spec(w) for w in ws],
        out_shape=[jax.ShapeDtypeStruct(w.shape, BF16) for w in ws],
        compiler_params=pltpu.CompilerParams(dimension_semantics=("arbitrary",)),
        name="cast_weights",
    )(*ws)


def _const_spec(shape):
    zeros = (0,) * len(shape)
    return pl.BlockSpec(shape, lambda i: zeros, pipeline_mode=pl.Buffered(1))


def _weight_specs():
    return [
        _const_spec((1, D_MODEL)),
        _const_spec((D_MODEL, W_IN)),
        _const_spec((CONV_A_WIDTH, W_A)),
        _const_spec((1, W_A)),
        _const_spec((1, W_A)),
        _const_spec((1, W_A)),
        _const_spec((CONV_B_WIDTH, W_B)),
        _const_spec((1, W_B)),
        _const_spec((N_B_HEADS, HEAD_B, 2 * HEAD_B)),
        _const_spec((1, W_B)),
        _const_spec((1, W_B)),
        _const_spec((1, W_B)),
        _const_spec((W_A + W_B, D_MODEL)),
        _const_spec((PLE_DIM, D_MODEL)),
        _const_spec((D_MODEL, D_MODEL)),
        _const_spec((1, D_MODEL)),
    ]


def _prompt_call(x, p, weights):
    nb, seq, _ = x.shape
    rows = TT * nb
    n_a = (CONV_A_WIDTH - 1) * nb
    n_b = (CONV_B_WIDTH - 1) * nb
    slab_rows = nb * SLAB_PITCH
    hbm = pl.BlockSpec(memory_space=pl.ANY)
    out_shape = (
        jax.ShapeDtypeStruct((nb, seq, D_MODEL), F32),
        jax.ShapeDtypeStruct((n_a, W_A), F32),
        jax.ShapeDtypeStruct((n_b, W_B), F32),
        jax.ShapeDtypeStruct((nb, W_B), F32),
    )
    const_out = lambda shape: pl.BlockSpec(shape, lambda i: (0, 0))
    return pl.pallas_call(
        functools.partial(_prompt_kernel, nb),
        grid=(seq // TT,),
        in_specs=[hbm, hbm] + _weight_specs(),
        out_specs=(hbm, const_out((n_a, W_A)), const_out((n_b, W_B)), const_out((nb, W_B))),
        out_shape=out_shape,
        scratch_shapes=[
            pltpu.VMEM((rows, D_MODEL), BF16),
            pltpu.VMEM((NC_GLU, rows, MXU_N), F32),
            pltpu.VMEM((NC_REST, HIST_B * nb + rows, MXU_N), F32),
            pltpu.VMEM(((HIST_A + TT) * nb, W_A), F32),
            pltpu.VMEM((rows, D_MODEL), F32),
            pltpu.VMEM((rows, W_B), F32),
            pltpu.VMEM((rows, W_A), BF16),
            pltpu.VMEM((rows, W_B), BF16),
            pltpu.VMEM((N_B_HEADS, rows, HEAD_B), BF16),
            pltpu.VMEM((nb, W_B), F32),
            pltpu.VMEM(((CONV_A_WIDTH + CONV_PAIRS) * nb, W_A), F32),
            pltpu.VMEM((2, D_MODEL // LANES, slab_rows, LANES), F32),
            pltpu.VMEM((2, PLE_DIM // LANES, slab_rows, LANES), F32),
            pltpu.VMEM((D_MODEL // LANES, slab_rows, LANES), F32),
            pltpu.SemaphoreType.DMA((2,)),
            pltpu.SemaphoreType.DMA((1,)),
        ],
        compiler_params=pltpu.CompilerParams(
            dimension_semantics=("arbitrary",), vmem_limit_bytes=VMEM_LIMIT_BYTES),
        name="prompt_layer",
    )(x, p, *weights)


def _sample_call(x_s, p_s, sa, sb_t, h0, weights):
    n = x_s.shape[0]
    ka = CONV_A_WIDTH - 1
    kb = CONV_B_WIDTH - 1
    out_shape = (
        jax.ShapeDtypeStruct((n, D_MODEL), F32),
        jax.ShapeDtypeStruct((ka, n, W_A), F32),
        jax.ShapeDtypeStruct((kb, n, W_B), F32),
        jax.ShapeDtypeStruct((n, W_B), F32),
    )
    full2 = lambda shape: pl.BlockSpec(shape, lambda i: (0, 0))
    full3 = lambda shape: pl.BlockSpec(shape, lambda i: (0, 0, 0))
    return pl.pallas_call(
        _sample_kernel,
        grid=(n // SAMPLE_BT,),
        in_specs=[full2((n, D_MODEL)), full2((n, PLE_DIM)),
                  pl.BlockSpec((ka, SAMPLE_BT, W_A), lambda i: (0, i, 0)),
                  full3((kb, n, W_B)), full2((n, W_B))] + _weight_specs(),
        out_specs=(full2((n, D_MODEL)),
                   pl.BlockSpec((ka, SAMPLE_BT, W_A), lambda i: (0, i, 0)),
                   full3((kb, n, W_B)), full2((n, W_B))),
        out_shape=out_shape,
        scratch_shapes=[
            pltpu.VMEM((n, D_MODEL), BF16),
            pltpu.VMEM((n, W_IN - 2 * W_A), F32),
            pltpu.VMEM((n, W_A), F32),
            pltpu.VMEM((n, W_A), F32),
        ],
        compiler_params=pltpu.CompilerParams(
            dimension_semantics=("arbitrary",), vmem_limit_bytes=VMEM_LIMIT_BYTES),
        name="sample_layer",
    )(x_s, p_s, sa, sb_t, h0, *weights)


def kernel(x_prompt, x_sample, p_prompt, p_sample, state_conv_a, state_conv_b, state_h, g_norm, w_in, w_dw_a, b_dw_a, ln_g, ln_b, w_conv_b, b_conv_b, w_r, b_r, w_i, b_i, lam, w_out, w_pe, w_pg, g_final):
    depth = g_norm.shape[0]
    assert depth == 1, "single-layer step"
    nb, seq, _ = x_prompt.shape
    assert nb == SUBLANES and seq % TT == 0 and TT >= HIST_A
    row = lambda a: a.reshape(1, -1)
    win16, wout16, wpe16, wpg16 = _cast_weights(w_in[0], w_out[0], w_pe[0], w_pg[0])
    weights = (
        row(g_norm[0]), win16,
        w_dw_a[0], row(b_dw_a[0]), row(ln_g[0]), row(ln_b[0]),
        w_conv_b[0], row(b_conv_b[0]),
        jnp.concatenate([w_r[0], w_i[0]], axis=-1).astype(BF16), row(b_r[0]), row(b_i[0]), row(lam[0]),
        wout16, wpe16, wpg16, row(g_final),
    )

    y_prompt, na_t, nbuf_t, nh_p = _prompt_call(x_prompt, p_prompt[0], weights)
    untime = lambda a, k: jnp.swapaxes(a.reshape(k, nb, a.shape[-1]), 0, 1)
    na_p = untime(na_t, CONV_A_WIDTH - 1)[None]
    nb_p = untime(nbuf_t, CONV_B_WIDTH - 1)[None]

    n_s = x_sample.shape[0]
    assert x_sample.shape[1] == 1 and n_s % SAMPLE_BT == 0
    y_s, na_s, nbuf_s, nh_s = _sample_call(
        x_sample.reshape(n_s, D_MODEL), p_sample[0].reshape(n_s, PLE_DIM), jnp.swapaxes(state_conv_a[0], 0, 1),
        jnp.swapaxes(state_conv_b[0], 0, 1), state_h[0], weights)
    return (y_prompt, y_s.reshape(n_s, 1, D_MODEL), na_p, nb_p, nh_p[None],
            jnp.swapaxes(na_s, 0, 1)[None], jnp.swapaxes(nbuf_s, 0, 1)[None], nh_s[None])
```
